```python
import jax, jax.numpy as jnp
from jax import lax
import numpy as np

D_MODEL = 4096
BATCH = 8
SEQ = 2048
DEPTH = 4
DEC_BATCH = 32
DEC_SEQ = 64
PAST_LEN = 1024

CHUNK = 64
EPS = 1e-6
A_HEADS = 32
A_KV_HEADS = 4
A_GROUP = A_HEADS // A_KV_HEADS
A_HEAD_DIM = 64
WINDOW = 128
WIN_CHUNKS = WINDOW // CHUNK
A_WIDTH = A_HEADS * A_HEAD_DIM
A_KV_WIDTH = A_KV_HEADS * A_HEAD_DIM
B_HEADS = 16
B_DK = 128
B_DV = 128
CONV_W = 4
B_QK_WIDTH = B_HEADS * B_DK
B_WIDTH = B_HEADS * B_DV
B_CONV_CH = 2 * B_QK_WIDTH + B_WIDTH
MIX_WIDTH = A_WIDTH + B_WIDTH
IN_SIZES = (A_WIDTH, A_KV_WIDTH, A_KV_WIDTH, B_CONV_CH, B_HEADS, B_HEADS, B_WIDTH)
IN_WIDTH = sum(IN_SIZES)
IN_SPLITS = tuple(int(s) for s in np.cumsum(IN_SIZES)[:-1])
N_MEM = 256
M_HEADS = 4
M_HEAD_DIM = 128
M_WIDTH = M_HEADS * M_HEAD_DIM
P_HEADS = 8
N_KEYS = 128
N_EXPERTS = N_KEYS * N_KEYS
P_DKEY = 256
P_HALF = P_DKEY // 2
P_TOPK = 16
P_BLOCK = 64

kernel_name = "hybrid_swa_gdn_peer_stream_step"


def rmsnorm(x, g):
    xf = x.astype(jnp.float32)
    y = xf * lax.rsqrt(jnp.mean(xf * xf, axis=-1, keepdims=True) + EPS)
    return (y * g.astype(jnp.float32)).astype(x.dtype)


def l2norm(x):
    return x * lax.rsqrt(jnp.sum(x * x, axis=-1, keepdims=True) + EPS)


def alibi_slopes():
    return jnp.exp2(-8.0 * jnp.arange(1, A_HEADS + 1, dtype=jnp.float32) / A_HEADS)


def sink_attention(q, k, v, dist, valid, sinks):
    slopes = alibi_slopes().reshape(A_KV_HEADS, A_GROUP, 1, 1)
    s = jnp.einsum('...qhgd,...jhd->...hgqj', q, k).astype(jnp.float32) * (A_HEAD_DIM ** -0.5)
    s = jnp.where(valid, s - slopes * dist, -jnp.inf)
    sk = sinks.astype(jnp.float32).reshape(A_KV_HEADS, A_GROUP, 1, 1)
    m = jnp.maximum(jnp.max(s, axis=-1, keepdims=True), sk)
    p = jnp.exp(s - m)
    p = p / (jnp.sum(p, axis=-1, keepdims=True) + jnp.exp(sk - m))
    return jnp.einsum('...hgqj,...jhd->...qhgd', p.astype(v.dtype), v)


def swa_prompt(q, k, v, sinks):
    B, S = q.shape[:2]
    nc = S // CHUNK
    kb_len = (WIN_CHUNKS + 1) * CHUNK
    qc = q.reshape(B, nc, CHUNK, A_KV_HEADS, A_GROUP, A_HEAD_DIM)

    def band(a):
        ac = a.reshape(B, nc, CHUNK, A_KV_HEADS, A_HEAD_DIM)
        ap = jnp.concatenate([jnp.zeros((B, WIN_CHUNKS) + ac.shape[2:], a.dtype), ac], axis=1)
        return jnp.concatenate([ap[:, j:j + nc] for j in range(WIN_CHUNKS + 1)], axis=2)

    kb, vb = band(k), band(v)
    i = jnp.arange(CHUNK)[:, None]
    j = jnp.arange(kb_len)[None, :]
    dist = jnp.abs(i + WINDOW - j).astype(jnp.float32)
    key_pos = jnp.arange(nc)[:, None] * CHUNK - WINDOW + jnp.arange(kb_len)[None, :]
    valid = (key_pos >= 0)[:, None, None, None, :]
    o = sink_attention(qc, kb, vb, dist, valid, sinks)
    keep = min(WINDOW, S)
    return o.reshape(B, S, A_WIDTH), k[:, S - keep:], v[:, S - keep:]


def swa_sample(q, k, v, sinks, ck, cv):
    B, T = q.shape[:2]
    W = ck.shape[1]
    kk = jnp.concatenate([ck.astype(k.dtype), k], axis=1)
    vv = jnp.concatenate([cv.astype(v.dtype), v], axis=1)
    qg = q.reshape(B, T, A_KV_HEADS, A_GROUP, A_HEAD_DIM)
    i = jnp.arange(T)[:, None]
    j = jnp.arange(W + T)[None, :]
    dist = jnp.abs(i + W - j).astype(jnp.float32)
    o = sink_attention(qg, kk, vv, dist, True, sinks)
    return o.reshape(B, T, A_WIDTH), kk[:, T:], vv[:, T:]


def short_conv(xin, prefix, w):
    L = xin.shape[1]
    xp = jnp.concatenate([prefix.astype(xin.dtype), xin], axis=1)
    y = sum(xp[:, j:j + L] * w[j] for j in range(CONV_W))
    return jax.nn.silu(y), xp[:, L:]


def gated_delta(q, k, v, g, beta, S0, chunk):
    B, L, H, _ = q.shape
    n = L // chunk

    def chunks(a):
        return jnp.moveaxis(a.reshape((B, n, chunk) + a.shape[2:]), 1, 0)

    tri_incl = jnp.tril(jnp.ones((chunk, chunk), bool))
    tri_strict = jnp.tril(jnp.ones((chunk, chunk), bool), -1)
    eye = jnp.eye(chunk, dtype=jnp.float32)

    def step(S, inp):
        qc, kc, vc, gc, bc = inp
        gcum = jnp.cumsum(gc, axis=1)
        gh = jnp.swapaxes(gcum, 1, 2)
        decay = jnp.exp(jnp.where(tri_incl, gh[..., :, None] - gh[..., None, :], -jnp.inf))
        kbeta = kc * bc[..., None]
        A = jnp.where(tri_strict, jnp.einsum('bihd,bjhd->bhij', kbeta, kc) * decay, 0.0)
        rhs = jnp.concatenate([jnp.swapaxes(vc * bc[..., None], 1, 2),
                               jnp.swapaxes(kbeta * jnp.exp(gcum)[..., None], 1, 2)], axis=-1)
        sol = lax.linalg.triangular_solve(eye + A, rhs, left_side=True, lower=True)
        u, w = sol[..., :B_DV], sol[..., B_DV:]
        v_new = u - jnp.einsum('bhcd,bhde->bhce', w, S)
        qh = jnp.swapaxes(qc, 1, 2)
        kh = jnp.swapaxes(kc, 1, 2)
        attn = jnp.einsum('bhid,bhjd->bhij', qh, kh) * decay
        o = (jnp.einsum('bhcd,bhde->bhce', qh * jnp.exp(gh)[..., None], S)
             + jnp.einsum('bhij,bhje->bhie', attn, v_new))
        g_last = gh[..., -1]
        S = (S * jnp.exp(g_last)[..., None, None]
             + jnp.einsum('bhcd,bhce->bhde', kh * jnp.exp(g_last[..., None] - gh)[..., None], v_new))
        return S, jnp.swapaxes(o, 1, 2)

    S, o = lax.scan(step, S0, (chunks(q), chunks(k), chunks(v), chunks(g), chunks(beta)))
    return jnp.moveaxis(o, 0, 1).reshape(B, L, H, B_DV), S


def gdn_mixer(qkv, a, b, z, conv_prefix, S0, conv_w, a_log, dt_bias, norm_g, chunk):
    B, L = qkv.shape[:2]
    act, conv_state = short_conv(qkv, conv_prefix, conv_w)
    act = act.astype(jnp.float32)
    q = l2norm(act[..., :B_QK_WIDTH].reshape(B, L, B_HEADS, B_DK)) * (B_DK ** -0.5)
    k = l2norm(act[..., B_QK_WIDTH:2 * B_QK_WIDTH].reshape(B, L, B_HEADS, B_DK))
    v = act[..., 2 * B_QK_WIDTH:].reshape(B, L, B_HEADS, B_DV)
    g = -jnp.exp(a_log.astype(jnp.float32)) * jax.nn.softplus(a.astype(jnp.float32) + dt_bias.astype(jnp.float32))
    beta = jax.nn.sigmoid(b.astype(jnp.float32))
    o, S = gated_delta(q, k, v, g, beta, S0.astype(jnp.float32), chunk)
    o = rmsnorm(o, norm_g) * jax.nn.silu(z.reshape(B, L, B_HEADS, B_DV).astype(jnp.float32))
    return o.reshape(B, L, B_WIDTH).astype(qkv.dtype), conv_state, S.astype(S0.dtype)


def parallel_mixer(h, w_in_l, w_out_l, conv_w_l, a_log_l, dt_bias_l, gdn_norm_l, sinks_l,
                   attn_fn, conv_prefix, S0, chunk):
    B, L, _ = h.shape
    z = h @ w_in_l
    aq, ak, av, bqkv, ba, bb, bz = jnp.split(z, IN_SPLITS, axis=-1)
    a_out, kbuf, vbuf = attn_fn(aq.reshape(B, L, A_HEADS, A_HEAD_DIM),
                                ak.reshape(B, L, A_KV_HEADS, A_HEAD_DIM),
                                av.reshape(B, L, A_KV_HEADS, A_HEAD_DIM), sinks_l)
    b_out, conv_state, S = gdn_mixer(bqkv, ba, bb, bz, conv_prefix, S0, conv_w_l,
                                     a_log_l, dt_bias_l, gdn_norm_l, chunk)
    y = jnp.concatenate([a_out, b_out.astype(a_out.dtype)], axis=-1) @ w_out_l
    return y, kbuf, vbuf, conv_state, S


def memory_kv(mem, ln_mem_l, wk, wv):
    B = mem.shape[0]
    m = rmsnorm(mem, ln_mem_l)
    return ((m @ wk).reshape(B, N_MEM, M_HEADS, M_HEAD_DIM),
            (m @ wv).reshape(B, N_MEM, M_HEADS, M_HEAD_DIM))


def cross_attend(h, mk, mv, wq, wo):
    B, L, _ = h.shape
    q = (h @ wq).reshape(B, L, M_HEADS, M_HEAD_DIM)
    s = jnp.einsum('bqhd,bkhd->bhqk', q, mk.astype(q.dtype)).astype(jnp.float32) * (M_HEAD_DIM ** -0.5)
    p = jax.nn.softmax(s, axis=-1)
    o = jnp.einsum('bhqk,bkhd->bqhd', p.astype(q.dtype), mv.astype(q.dtype)).reshape(B, L, M_WIDTH)
    return o @ wo


def peer(h, wq, sk1, sk2, u, v):
    B, L, D = h.shape
    q = (h @ wq).reshape(B, L, P_HEADS, P_DKEY).astype(jnp.float32)
    s1 = jnp.einsum('blhd,hnd->blhn', q[..., :P_HALF], sk1.astype(jnp.float32))
    s2 = jnp.einsum('blhd,hnd->blhn', q[..., P_HALF:], sk2.astype(jnp.float32))
    t1, i1 = lax.top_k(s1, P_TOPK)
    t2, i2 = lax.top_k(s2, P_TOPK)
    cand = (t1[..., :, None] + t2[..., None, :]).reshape(B, L, P_HEADS, P_TOPK * P_TOPK)
    cidx = (i1[..., :, None] * N_KEYS + i2[..., None, :]).reshape(B, L, P_HEADS, P_TOPK * P_TOPK)
    best, pos = lax.top_k(cand, P_TOPK)
    idx = jnp.take_along_axis(cidx, pos, axis=-1)
    gate = jax.nn.softmax(best, axis=-1)
    n = B * L
    nb = -(-n // P_BLOCK)
    padn = nb * P_BLOCK - n
    E = P_HEADS * P_TOPK
    hf = jnp.pad(h.reshape(n, D), ((0, padn), (0, 0))).reshape(nb, P_BLOCK, D)
    idf = jnp.pad(idx.reshape(n, E), ((0, padn), (0, 0))).reshape(nb, P_BLOCK, E)
    gf = jnp.pad(gate.reshape(n, E), ((0, padn), (0, 0))).reshape(nb, P_BLOCK, E)

    def block(args):
        hb, ib, gb = args
        ue = jnp.take(u, ib, axis=0)
        ve = jnp.take(v, ib, axis=0)
        act = jax.nn.gelu(jnp.einsum('td,ted->te', hb, ue.astype(hb.dtype)).astype(jnp.float32),
                          approximate=False)
        return jnp.einsum('te,ted->td', (act * gb).astype(hb.dtype), ve.astype(hb.dtype))

    out = lax.map(block, (hf, idf, gf)).reshape(nb * P_BLOCK, D)[:n]
    return out.reshape(B, L, D)


def setup_inputs(seed: int = 0) -> dict:
    key = jax.random.key(seed)
    ks = jax.random.split(key, 32)

    def nrm(i, shape, scale):
        return jax.random.normal(ks[i], shape, jnp.float32) * scale

    wlen = min(WINDOW, PAST_LEN)
    dt = jax.random.uniform(ks[13], (DEPTH, B_HEADS), jnp.float32, 0.001, 0.1)
    return {
        "x_prompt": nrm(0, (BATCH, SEQ, D_MODEL), 1.0),
        "x_sample": nrm(1, (DEC_BATCH, DEC_SEQ, D_MODEL), 1.0),
        "mem_prompt": nrm(2, (BATCH, N_MEM, D_MODEL), 1.0),
        "cache_swa_k": nrm(3, (DEPTH, DEC_BATCH, wlen, A_KV_HEADS, A_HEAD_DIM), 1.0),
        "cache_swa_v": nrm(4, (DEPTH, DEC_BATCH, wlen, A_KV_HEADS, A_HEAD_DIM), 1.0),
        "state_conv": nrm(5, (DEPTH, DEC_BATCH, CONV_W - 1, B_CONV_CH), 1.0),
        "state_gdn": nrm(6, (DEPTH, DEC_BATCH, B_HEADS, B_DK, B_DV), 0.5),
        "cache_mem_k": nrm(7, (DEPTH, DEC_BATCH, N_MEM, M_HEADS, M_HEAD_DIM), 1.0),
        "cache_mem_v": nrm(8, (DEPTH, DEC_BATCH, N_MEM, M_HEADS, M_HEAD_DIM), 1.0),
        "ln_mix": 1.0 + nrm(9, (DEPTH, D_MODEL), 0.02),
        "w_in": nrm(10, (DEPTH, D_MODEL, IN_WIDTH), D_MODEL ** -0.5),
        "conv_w": nrm(11, (DEPTH, CONV_W, B_CONV_CH), CONV_W ** -0.5),
        "a_log": jnp.log(jax.random.uniform(ks[12], (DEPTH, B_HEADS), jnp.float32, 1.0, 16.0)),
        "dt_bias": jnp.log(jnp.expm1(dt)),
        "gdn_norm": 1.0 + nrm(14, (DEPTH, B_DV), 0.02),
        "sinks": nrm(15, (DEPTH, A_HEADS), 0.5),
        "w_out": nrm(16, (DEPTH, MIX_WIDTH, D_MODEL), MIX_WIDTH ** -0.5),
        "ln_cross": 1.0 + nrm(17, (DEPTH, D_MODEL), 0.02),
        "ln_mem": 1.0 + nrm(18, (DEPTH, D_MODEL), 0.02),
        "w_mq": nrm(19, (DEPTH, D_MODEL, M_WIDTH), D_MODEL ** -0.5),
        "w_mk": nrm(20, (DEPTH, D_MODEL, M_WIDTH), D_MODEL ** -0.5),
        "w_mv": nrm(21, (DEPTH, D_MODEL, M_WIDTH), D_MODEL ** -0.5),
        "w_mo": nrm(22, (DEPTH, M_WIDTH, D_MODEL), M_WIDTH ** -0.5),
        "ln_ffn": 1.0 + nrm(23, (DEPTH, D_MODEL), 0.02),
        "w_pq": nrm(24, (DEPTH, D_MODEL, P_HEADS * P_DKEY), D_MODEL ** -0.5),
        "sub_keys1": nrm(25, (DEPTH, P_HEADS, N_KEYS, P_HALF), P_HALF ** -0.5),
        "sub_keys2": nrm(26, (DEPTH, P_HEADS, N_KEYS, P_HALF), P_HALF ** -0.5),
        "expert_u": nrm(27, (DEPTH, N_EXPERTS, D_MODEL), D_MODEL ** -0.5),
        "expert_v": nrm(28, (DEPTH, N_EXPERTS, D_MODEL), (P_HEADS * P_TOPK) ** -0.5),
        "ln_final": 1.0 + nrm(29, (D_MODEL,), 0.02),
    }


def reference(x_prompt, x_sample, mem_prompt, cache_swa_k, cache_swa_v, state_conv, state_gdn,
              cache_mem_k, cache_mem_v, ln_mix, w_in, conv_w, a_log, dt_bias, gdn_norm, sinks, w_out,
              ln_cross, ln_mem, w_mq, w_mk, w_mv, w_mo, ln_ffn, w_pq, sub_keys1, sub_keys2,
              expert_u, expert_v, ln_final):
    xp, xs = x_prompt, x_sample
    Bp = xp.shape[0]
    T = xs.shape[1]
    swa_k_p, swa_v_p, conv_p, gdn_p, mem_k_p, mem_v_p = [], [], [], [], [], []
    swa_k_s, swa_v_s, conv_s, gdn_s = [], [], [], []
    for l in range(DEPTH):
        yp, kb, vb, cst, S = parallel_mixer(
            rmsnorm(xp, ln_mix[l]), w_in[l], w_out[l], conv_w[l], a_log[l], dt_bias[l], gdn_norm[l], sinks[l],
            swa_prompt, jnp.zeros((Bp, CONV_W - 1, B_CONV_CH), xp.dtype),
            jnp.zeros((Bp, B_HEADS, B_DK, B_DV), jnp.float32), CHUNK)
        xp = xp + yp
        mk, mv = memory_kv(mem_prompt, ln_mem[l], w_mk[l], w_mv[l])
        xp = xp + cross_attend(rmsnorm(xp, ln_cross[l]), mk, mv, w_mq[l], w_mo[l])
        xp = xp + peer(rmsnorm(xp, ln_ffn[l]), w_pq[l], sub_keys1[l], sub_keys2[l], expert_u[l], expert_v[l])
        swa_k_p.append(kb); swa_v_p.append(vb); conv_p.append(cst); gdn_p.append(S)
        mem_k_p.append(mk); mem_v_p.append(mv)
        ck, cv = cache_swa_k[l], cache_swa_v[l]
        ys, kb, vb, cst, S = parallel_mixer(
            rmsnorm(xs, ln_mix[l]), w_in[l], w_out[l], conv_w[l], a_log[l], dt_bias[l], gdn_norm[l], sinks[l],
            lambda q, k, v, s: swa_sample(q, k, v, s, ck, cv), state_conv[l], state_gdn[l], T)
        xs = xs + ys
        xs = xs + cross_attend(rmsnorm(xs, ln_cross[l]), cache_mem_k[l], cache_mem_v[l], w_mq[l], w_mo[l])
        xs = xs + peer(rmsnorm(xs, ln_ffn[l]), w_pq[l], sub_keys1[l], sub_keys2[l], expert_u[l], expert_v[l])
        swa_k_s.append(kb); swa_v_s.append(vb); conv_s.append(cst); gdn_s.append(S)
    y_prompt = rmsnorm(xp, ln_final)
    y_sample = rmsnorm(xs, ln_final)
    return (y_prompt, y_sample,
            jnp.stack(swa_k_p), jnp.stack(swa_v_p), jnp.stack(conv_p), jnp.stack(gdn_p),
            jnp.stack(mem_k_p), jnp.stack(mem_v_p),
            jnp.stack(swa_k_s), jnp.stack(swa_v_s), jnp.stack(conv_s), jnp.stack(gdn_s))
```

```python
import functools

import jax
import jax.numpy as jnp
import numpy as np
from jax import lax
from jax.experimental import pallas as pl
from jax.experimental.pallas import tpu as pltpu

F32 = jnp.float32
BF16 = jnp.bfloat16

D_MODEL = 4096
DEPTH = 4
CHUNK = 64
EPS = 1e-6
A_HEADS = 32
A_KV_HEADS = 4
A_GROUP = A_HEADS // A_KV_HEADS
A_HEAD_DIM = 64
WINDOW = 128
WIN_CHUNKS = WINDOW // CHUNK
A_WIDTH = A_HEADS * A_HEAD_DIM
A_KV_WIDTH = A_KV_HEADS * A_HEAD_DIM
B_HEADS = 16
B_DK = 128
B_DV = 128
CONV_W = 4
B_QK_WIDTH = B_HEADS * B_DK
B_WIDTH = B_HEADS * B_DV
B_CONV_CH = 2 * B_QK_WIDTH + B_WIDTH
N_MEM = 256
M_HEADS = 4
M_HEAD_DIM = 128
M_WIDTH = M_HEADS * M_HEAD_DIM
P_HEADS = 8
N_KEYS = 128
N_EXPERTS = N_KEYS * N_KEYS
P_DKEY = 256
P_HALF = P_DKEY // 2
P_TOPK = 16

Z_AQ = 0
Z_AK = Z_AQ + A_WIDTH
Z_AV = Z_AK + A_KV_WIDTH
Z_QKV = Z_AV + A_KV_WIDTH
Z_Z = Z_QKV + B_CONV_CH
Z_A = Z_Z + B_WIDTH
Z_B = Z_A + B_HEADS
Z_USED = Z_B + B_HEADS
Z_WIDTH = 11264

V7X_VMEM_BYTES = 64 * 1024 * 1024
VMEM_LIMIT = V7X_VMEM_BYTES - 8 * 1024 * 1024


def _cparams(sem):
    return pltpu.CompilerParams(dimension_semantics=sem, vmem_limit_bytes=VMEM_LIMIT)


def _norm_matmul_kernel(x_ref, g_ref, w_ref, o_ref, *rest, emit_hn):
    hn_ref = rest[-1]
    j = pl.program_id(1)

    @pl.when(j == 0)
    def _():
        x = x_ref[...]
        ms = jnp.mean(x * x, axis=-1, keepdims=True)
        hn = (x * lax.rsqrt(ms + EPS) * g_ref[...]).astype(BF16)
        hn_ref[...] = hn
        if emit_hn:
            rest[0][...] = hn

    o_ref[...] = jnp.dot(hn_ref[...], w_ref[...], preferred_element_type=F32)


def norm_matmul(x, g, w, *, tm, tn, emit_hn=False):
    t, d = x.shape
    n = w.shape[1]
    out_shape = [jax.ShapeDtypeStruct((t, n), F32)]
    out_specs = [pl.BlockSpec((tm, tn), lambda i, j: (i, j))]
    if emit_hn:
        out_shape.append(jax.ShapeDtypeStruct((t, d), BF16))
        out_specs.append(pl.BlockSpec((tm, d), lambda i, j: (i, 0)))
    res = pl.pallas_call(
        functools.partial(_norm_matmul_kernel, emit_hn=emit_hn),
        grid=(t // tm, n // tn),
        in_specs=[pl.BlockSpec((tm, d), lambda i, j: (i, 0)),
                  pl.BlockSpec((1, d), lambda i, j: (0, 0)),
                  pl.BlockSpec((d, tn), lambda i, j: (0, j))],
        out_specs=out_specs,
        out_shape=out_shape,
        scratch_shapes=[pltpu.VMEM((tm, d), BF16)],
        compiler_params=_cparams(("parallel", "arbitrary")),
        name="norm_matmul",
    )(x, g.reshape(1, d), w)
    return res if emit_hn else res[0]


def _matmul_res_kernel(x_ref, a_ref, w_ref, o_ref):
    o_ref[...] = x_ref[...] + jnp.dot(a_ref[...], w_ref[...], preferred_element_type=F32)


def matmul_residual(x, a, w, *, tm, tn):
    t, n = x.shape
    k = a.shape[1]
    return pl.pallas_call(
        _matmul_res_kernel,
        grid=(t // tm, n // tn),
        in_specs=[pl.BlockSpec((tm, tn), lambda i, j: (i, j)),
                  pl.BlockSpec((tm, k), lambda i, j: (i, 0)),
                  pl.BlockSpec((k, tn), lambda i, j: (0, j))],
        out_specs=pl.BlockSpec((tm, tn), lambda i, j: (i, j)),
        out_shape=jax.ShapeDtypeStruct((t, n), F32),
        compiler_params=_cparams(("parallel", "arbitrary")),
        name="matmul_residual",
    )(x, a, w)


def _rmsnorm_kernel(x_ref, g_ref, o_ref):
    x = x_ref[...]
    ms = jnp.mean(x * x, axis=-1, keepdims=True)
    o_ref[...] = x * lax.rsqrt(ms + EPS) * g_ref[...]


def rmsnorm_rows(x, g, *, tm):
    t, d = x.shape
    return pl.pallas_call(
        _rmsnorm_kernel,
        grid=(t // tm,),
        in_specs=[pl.BlockSpec((tm, d), lambda i: (i, 0)),
                  pl.BlockSpec((1, d), lambda i: (0, 0))],
        out_specs=pl.BlockSpec((tm, d), lambda i: (i, 0)),
        out_shape=jax.ShapeDtypeStruct((t, d), F32),
        compiler_params=_cparams(("parallel",)),
        name="final_rmsnorm",
    )(x, g.reshape(1, d))


N_CAND = 50
CAND_ROWS = 56


def _remove_first(cur, mk, iota, sentinel):
    idx = jnp.min(jnp.where(cur == mk, iota, sentinel), axis=0, keepdims=True)
    return jnp.where(iota == idx, -jnp.inf, cur)


def _peer_select_kernel(q_ref, sk1_ref, sk2_ref, s1_ref, s2_ref, tau_ref, m_ref, iz_ref, cand_ref):
    tm = q_ref.shape[0]
    iota_k = lax.broadcasted_iota(jnp.int32, (N_KEYS, tm), 0)
    iota_c = lax.broadcasted_iota(jnp.int32, (CAND_ROWS, tm), 0)
    nt = (((1,), (1,)), ((), ()))

    def top16(s):
        vals = []
        cur = s
        for k in range(P_TOPK):
            mk = jnp.max(cur, axis=0, keepdims=True)
            vals.append(mk)
            if k + 1 < P_TOPK:
                cur = _remove_first(cur, mk, iota_k, N_KEYS)
        return vals

    cand_ref[...] = jnp.full((CAND_ROWS, tm), -jnp.inf, F32)
    for h in range(P_HEADS):
        q1 = q_ref[:, h * P_DKEY:h * P_DKEY + P_HALF].astype(BF16)
        q2 = q_ref[:, h * P_DKEY + P_HALF:(h + 1) * P_DKEY].astype(BF16)
        s1 = lax.dot_general(sk1_ref[h], q1, nt, preferred_element_type=F32)
        s2 = lax.dot_general(sk2_ref[h], q2, nt, preferred_element_type=F32)
        s1_ref[h] = s1
        s2_ref[h] = s2
        t1 = top16(s1)
        t2 = top16(s2)
        r = 0
        for a in range(P_TOPK):
            for b in range(P_TOPK):
                if (a + 1) * (b + 1) <= P_TOPK:
                    cand_ref[r:r + 1, :] = t1[a] + t2[b]
                    r += 1
        assert r == N_CAND
        cur = cand_ref[...]
        m = t1[0] + t2[0]
        z = jnp.zeros((1, tm), F32)
        mk = m
        for k in range(P_TOPK):
            mk = jnp.max(cur, axis=0, keepdims=True)
            z = z + jnp.exp(mk - m)
            if k + 1 < P_TOPK:
                cur = _remove_first(cur, mk, iota_c, CAND_ROWS)
        tau_ref[h:h + 1, :] = mk
        m_ref[h:h + 1, :] = m
        iz_ref[h:h + 1, :] = 1.0 / z


def peer_select(q, sk1, sk2, *, tm):
    t = q.shape[0]
    big = jax.ShapeDtypeStruct((P_HEADS, N_KEYS, t), F32)
    small = jax.ShapeDtypeStruct((P_HEADS, t), F32)
    big_spec = pl.BlockSpec((P_HEADS, N_KEYS, tm), lambda i: (0, 0, i))
    small_spec = pl.BlockSpec((P_HEADS, tm), lambda i: (0, i))
    key_spec = pl.BlockSpec((P_HEADS, N_KEYS, P_HALF), lambda i: (0, 0, 0))
    return pl.pallas_call(
        _peer_select_kernel,
        grid=(t // tm,),
        in_specs=[pl.BlockSpec((tm, P_HEADS * P_DKEY), lambda i: (i, 0)), key_spec, key_spec],
        out_specs=[big_spec, big_spec, small_spec, small_spec, small_spec],
        out_shape=[big, big, small, small, small],
        scratch_shapes=[pltpu.VMEM((CAND_ROWS, tm), F32)],
        compiler_params=_cparams(("parallel",)),
        name="peer_select",
    )(q, sk1, sk2)


LANES = 128
SQRT_HALF = float(np.sqrt(0.5))


def _peer_main_kernel(hnT_ref, u_ref, vt_ref, s1_ref, s2_ref, tau_ref, m_ref, iz_ref, o_ref,
                      a_ref, p_ref):
    e = pl.program_id(1)
    te, tm = a_ref.shape
    rows = te // N_KEYS

    @pl.when(e == 0)
    def _():
        o_ref[...] = jnp.zeros_like(o_ref)

    a_ref[...] = jnp.dot(u_ref[...], hnT_ref[...], preferred_element_type=F32)

    def row_body(r, carry):
        i1 = e * rows + r
        r0 = pl.multiple_of(r * N_KEYS, N_KEYS)
        for c in range(tm // LANES):
            cs = slice(c * LANES, (c + 1) * LANES)
            g = jnp.zeros((N_KEYS, LANES), F32)
            for h in range(P_HEADS):
                s = s1_ref[i1, h:h + 1, cs] + s2_ref[h, :, cs]
                w = jnp.exp(s - m_ref[h:h + 1, cs]) * iz_ref[h:h + 1, cs]
                g = g + jnp.where(s >= tau_ref[h:h + 1, cs], w, 0.0)
            a = a_ref[pl.ds(r0, N_KEYS), cs]
            act = 0.5 * a * (1.0 + lax.erf(a * SQRT_HALF))
            p_ref[pl.ds(r0, N_KEYS), cs] = (act * g).astype(BF16)
        return carry

    lax.fori_loop(0, rows, row_body, 0)
    o_ref[...] += jnp.dot(vt_ref[...], p_ref[...], preferred_element_type=F32)


def peer_main(hnT, u, vt, s1, s2, tau, m, iz, *, tm, te):
    d, t = hnT.shape
    small_spec = pl.BlockSpec((P_HEADS, tm), lambda i, e: (0, i))
    big_spec = pl.BlockSpec((P_HEADS, N_KEYS, tm), lambda i, e: (0, 0, i))
    s1_spec = pl.BlockSpec((N_KEYS, P_HEADS, tm), lambda i, e: (0, 0, i))
    return pl.pallas_call(
        _peer_main_kernel,
        grid=(t // tm, N_EXPERTS // te),
        in_specs=[pl.BlockSpec((d, tm), lambda i, e: (0, i)),
                  pl.BlockSpec((te, d), lambda i, e: (e, 0)),
                  pl.BlockSpec((d, te), lambda i, e: (0, e)),
                  s1_spec, big_spec, small_spec, small_spec, small_spec],
        out_specs=pl.BlockSpec((d, tm), lambda i, e: (0, i)),
        out_shape=jax.ShapeDtypeStruct((d, t), F32),
        scratch_shapes=[pltpu.VMEM((te, tm), F32), pltpu.VMEM((te, tm), BF16)],
        compiler_params=_cparams(("parallel", "arbitrary")),
        name="peer_main",
    )(hnT, u, vt, jnp.transpose(s1, (1, 0, 2)), s2, tau, m, iz)


def _alibi_slopes():
    return jnp.exp2(-8.0 * jnp.arange(1, A_HEADS + 1, dtype=F32) / A_HEADS)


def _sink_attention(q, k, v, dist, valid, sinks):
    slopes = _alibi_slopes().reshape(A_KV_HEADS, A_GROUP, 1, 1)
    s = jnp.einsum('...qhgd,...jhd->...hgqj', q, k).astype(F32) * (A_HEAD_DIM ** -0.5)
    s = jnp.where(valid, s - slopes * dist, -jnp.inf)
    sk = sinks.astype(F32).reshape(A_KV_HEADS, A_GROUP, 1, 1)
    m = jnp.maximum(jnp.max(s, axis=-1, keepdims=True), sk)
    p = jnp.exp(s - m)
    p = p / (jnp.sum(p, axis=-1, keepdims=True) + jnp.exp(sk - m))
    return jnp.einsum('...hgqj,...jhd->...qhgd', p.astype(v.dtype), v)


def _swa_prompt(q, k, v, sinks):
    B, S = q.shape[:2]
    nc = S // CHUNK
    kb_len = (WIN_CHUNKS + 1) * CHUNK
    qc = q.reshape(B, nc, CHUNK, A_KV_HEADS, A_GROUP, A_HEAD_DIM)

    def band(a):
        ac = a.reshape(B, nc, CHUNK, A_KV_HEADS, A_HEAD_DIM)
        ap = jnp.concatenate([jnp.zeros((B, WIN_CHUNKS) + ac.shape[2:], a.dtype), ac], axis=1)
        return jnp.concatenate([ap[:, j:j + nc] for j in range(WIN_CHUNKS + 1)], axis=2)

    kb, vb = band(k), band(v)
    i = jnp.arange(CHUNK)[:, None]
    j = jnp.arange(kb_len)[None, :]
    dist = jnp.abs(i + WINDOW - j).astype(F32)
    key_pos = jnp.arange(nc)[:, None] * CHUNK - WINDOW + jnp.arange(kb_len)[None, :]
    valid = (key_pos >= 0)[:, None, None, None, :]
    o = _sink_attention(qc, kb, vb, dist, valid, sinks)
    keep = min(WINDOW, S)
    return o.reshape(B, S, A_WIDTH), k[:, S - keep:], v[:, S - keep:]


def _swa_sample(q, k, v, sinks, ck, cv):
    B, T = q.shape[:2]
    W = ck.shape[1]
    kk = jnp.concatenate([ck, k], axis=1)
    vv = jnp.concatenate([cv, v], axis=1)
    qg = q.reshape(B, T, A_KV_HEADS, A_GROUP, A_HEAD_DIM)
    i = jnp.arange(T)[:, None]
    j = jnp.arange(W + T)[None, :]
    dist = jnp.abs(i + W - j).astype(F32)
    o = _sink_attention(qg, kk, vv, dist, True, sinks)
    return o.reshape(B, T, A_WIDTH), kk[:, T:], vv[:, T:]


def _short_conv(xin, prefix, w):
    L = xin.shape[1]
    xp = jnp.concatenate([prefix, xin], axis=1)
    y = sum(xp[:, j:j + L] * w[j] for j in range(CONV_W))
    return jax.nn.silu(y), xp[:, L:]


def _l2norm(x):
    return x * lax.rsqrt(jnp.sum(x * x, axis=-1, keepdims=True) + EPS)


def _gated_delta(q, k, v, g, beta, S0, chunk):
    B, L, H, _ = q.shape
    n = L // chunk

    def chunks(a):
        return jnp.moveaxis(a.reshape((B, n, chunk) + a.shape[2:]), 1, 0)

    tri_incl = jnp.tril(jnp.ones((chunk, chunk), bool))
    tri_strict = jnp.tril(jnp.ones((chunk, chunk), bool), -1)
    eye = jnp.eye(chunk, dtype=F32)

    def step(S, inp):
        qc, kc, vc, gc, bc = inp
        gcum = jnp.cumsum(gc, axis=1)
        gh = jnp.swapaxes(gcum, 1, 2)
        decay = jnp.exp(jnp.where(tri_incl, gh[..., :, None] - gh[..., None, :], -jnp.inf))
        kbeta = kc * bc[..., None]
        A = jnp.where(tri_strict, jnp.einsum('bihd,bjhd->bhij', kbeta, kc) * decay, 0.0)
        rhs = jnp.concatenate([jnp.swapaxes(vc * bc[..., None], 1, 2),
                               jnp.swapaxes(kbeta * jnp.exp(gcum)[..., None], 1, 2)], axis=-1)
        sol = lax.linalg.triangular_solve(eye + A, rhs, left_side=True, lower=True)
        u, w = sol[..., :B_DV], sol[..., B_DV:]
        v_new = u - jnp.einsum('bhcd,bhde->bhce', w, S)
        qh = jnp.swapaxes(qc, 1, 2)
        kh = jnp.swapaxes(kc, 1, 2)
        attn = jnp.einsum('bhid,bhjd->bhij', qh, kh) * decay
        o = (jnp.einsum('bhcd,bhde->bhce', qh * jnp.exp(gh)[..., None], S)
             + jnp.einsum('bhij,bhje->bhie', attn, v_new))
        g_last = gh[..., -1]
        S = (S * jnp.exp(g_last)[..., None, None]
             + jnp.einsum('bhcd,bhce->bhde', kh * jnp.exp(g_last[..., None] - gh)[..., None], v_new))
        return S, jnp.swapaxes(o, 1, 2)

    S, o = lax.scan(step, S0, (chunks(q), chunks(k), chunks(v), chunks(g), chunks(beta)))
    return jnp.moveaxis(o, 0, 1).reshape(B, L, H, B_DV), S


def _gdn_mixer(qkv, a, b, z, conv_prefix, S0, conv_w, a_log, dt_bias, norm_g, chunk):
    B, L = qkv.shape[:2]
    act, conv_state = _short_conv(qkv, conv_prefix, conv_w)
    q = _l2norm(act[..., :B_QK_WIDTH].reshape(B, L, B_HEADS, B_DK)) * (B_DK ** -0.5)
    k = _l2norm(act[..., B_QK_WIDTH:2 * B_QK_WIDTH].reshape(B, L, B_HEADS, B_DK))
    v = act[..., 2 * B_QK_WIDTH:].reshape(B, L, B_HEADS, B_DV)
    g = -jnp.exp(a_log) * jax.nn.softplus(a + dt_bias)
    beta = jax.nn.sigmoid(b)
    o, S = _gated_delta(q, k, v, g, beta, S0, chunk)
    ms = jnp.mean(o * o, axis=-1, keepdims=True)
    o = (o * lax.rsqrt(ms + EPS) * norm_g) * jax.nn.silu(z.reshape(B, L, B_HEADS, B_DV))
    return o.reshape(B, L, B_WIDTH), conv_state, S


def _mixer(zz, attn_fn, conv_prefix, S0, conv_w, a_log, dt_bias, norm_g, sinks, chunk):
    B, L = zz.shape[:2]
    aq = zz[..., Z_AQ:Z_AK].reshape(B, L, A_HEADS, A_HEAD_DIM)
    ak = zz[..., Z_AK:Z_AV].reshape(B, L, A_KV_HEADS, A_HEAD_DIM)
    av = zz[..., Z_AV:Z_QKV].reshape(B, L, A_KV_HEADS, A_HEAD_DIM)
    a_out, kbuf, vbuf = attn_fn(aq, ak, av, sinks)
    b_out, conv_state, S = _gdn_mixer(zz[..., Z_QKV:Z_Z], zz[..., Z_A:Z_B], zz[..., Z_B:Z_USED],
                                      zz[..., Z_Z:Z_A], conv_prefix, S0, conv_w, a_log, dt_bias,
                                      norm_g, chunk)
    return jnp.concatenate([a_out, b_out], axis=-1), kbuf, vbuf, conv_state, S


def _cross_core(q, mk, mv):
    B, L = q.shape[:2]
    q = q.reshape(B, L, M_HEADS, M_HEAD_DIM)
    s = jnp.einsum('bqhd,bkhd->bhqk', q, mk).astype(F32) * (M_HEAD_DIM ** -0.5)
    p = jax.nn.softmax(s, axis=-1)
    return jnp.einsum('bhqk,bkhd->bqhd', p, mv).reshape(B, L, M_WIDTH)


def kernel(x_prompt, x_sample, mem_prompt, cache_swa_k, cache_swa_v, state_conv, state_gdn, cache_mem_k, cache_mem_v, ln_mix, w_in, conv_w, a_log, dt_bias, gdn_norm, sinks, w_out, ln_cross, ln_mem, w_mq, w_mk, w_mv, w_mo, ln_ffn, w_pq, sub_keys1, sub_keys2, expert_u, expert_v, ln_final):
    Bp, Sp, D = x_prompt.shape
    Bs, Ts, _ = x_sample.shape
    n_p = Bp * Sp
    n_s = Bs * Ts
    x = jnp.concatenate([x_prompt.reshape(n_p, D), x_sample.reshape(n_s, D)], axis=0)
    mem = mem_prompt.reshape(Bp * N_MEM, D)

    outs = [[] for _ in range(10)]
    for l in range(DEPTH):
        wi = w_in[l]
        w_in_l = jnp.concatenate(
            [wi[:, :Z_Z], wi[:, Z_Z + 2 * B_HEADS:], wi[:, Z_Z:Z_Z + 2 * B_HEADS],
             jnp.zeros((D, Z_WIDTH - Z_USED), F32)], axis=1).astype(BF16)
        w_out_l = w_out[l].astype(BF16)
        w_mq_l = w_mq[l].astype(BF16)
        w_mkv_l = jnp.concatenate([w_mk[l], w_mv[l]], axis=1).astype(BF16)
        w_mo_l = w_mo[l].astype(BF16)
        w_pq_l = w_pq[l].astype(BF16)
        u_l = expert_u[l].astype(BF16)
        vt_l = expert_v[l].astype(BF16).T
        sk1_l = sub_keys1[l].astype(BF16)
        sk2_l = sub_keys2[l].astype(BF16)

        zz = norm_matmul(x, ln_mix[l], w_in_l, tm=512, tn=1024)
        zp = zz[:n_p].reshape(Bp, Sp, Z_WIDTH)
        zs = zz[n_p:].reshape(Bs, Ts, Z_WIDTH)
        mix_p, kb_p, vb_p, cst_p, S_p = _mixer(
            zp, _swa_prompt, jnp.zeros((Bp, CONV_W - 1, B_CONV_CH), F32),
            jnp.zeros((Bp, B_HEADS, B_DK, B_DV), F32), conv_w[l], a_log[l], dt_bias[l], gdn_norm[l],
            sinks[l], CHUNK)
        ck, cv = cache_swa_k[l], cache_swa_v[l]
        mix_s, kb_s, vb_s, cst_s, S_s = _mixer(
            zs, lambda q, k, v, s: _swa_sample(q, k, v, s, ck, cv), state_conv[l], state_gdn[l],
            conv_w[l], a_log[l], dt_bias[l], gdn_norm[l], sinks[l], Ts)
        mix = jnp.concatenate([mix_p.reshape(n_p, -1), mix_s.reshape(n_s, -1)], axis=0).astype(BF16)
        x = matmul_residual(x, mix, w_out_l, tm=512, tn=1024)

        mkv = norm_matmul(mem, ln_mem[l], w_mkv_l, tm=512, tn=1024)
        mk = mkv[:, :M_WIDTH].reshape(Bp, N_MEM, M_HEADS, M_HEAD_DIM)
        mv = mkv[:, M_WIDTH:].reshape(Bp, N_MEM, M_HEADS, M_HEAD_DIM)
        qm = norm_matmul(x, ln_cross[l], w_mq_l, tm=512, tn=M_WIDTH)
        o_p = _cross_core(qm[:n_p].reshape(Bp, Sp, M_WIDTH), mk, mv)
        o_s = _cross_core(qm[n_p:].reshape(Bs, Ts, M_WIDTH), cache_mem_k[l], cache_mem_v[l])
        o = jnp.concatenate([o_p.reshape(n_p, -1), o_s.reshape(n_s, -1)], axis=0).astype(BF16)
        x = matmul_residual(x, o, w_mo_l, tm=512, tn=1024)

        qp, hn = norm_matmul(x, ln_ffn[l], w_pq_l, tm=512, tn=512, emit_hn=True)
        s1, s2, tau, m, iz = peer_select(qp, sk1_l, sk2_l, tm=256)
        yT = peer_main(hn.T, u_l, vt_l, s1, s2, tau, m, iz, tm=512, te=512)
        x = x + yT.T

        for lst, val in zip(outs, (kb_p, vb_p, cst_p, S_p, mk, mv, kb_s, vb_s, cst_s, S_s)):
            lst.append(val)

    y = rmsnorm_rows(x, ln_final, tm=512)
    y_prompt = y[:n_p].reshape(Bp, Sp, D)
    y_sample = y[n_p:].reshape(Bs, Ts, D)
    return (y_prompt, y_sample) + tuple(jnp.stack(o) for o in outs)
```

```python
import functools

import jax
import jax.numpy as jnp
import numpy as np
from jax import lax
from jax.experimental import pallas as pl
from jax.experimental.pallas import tpu as pltpu

F32 = jnp.float32
BF16 = jnp.bfloat16

D_MODEL = 4096
DEPTH = 4
CHUNK = 64
EPS = 1e-6
A_HEADS = 32
A_KV_HEADS = 4
A_GROUP = A_HEADS // A_KV_HEADS
A_HEAD_DIM = 64
WINDOW = 128
WIN_CHUNKS = WINDOW // CHUNK
A_WIDTH = A_HEADS * A_HEAD_DIM
A_KV_WIDTH = A_KV_HEADS * A_HEAD_DIM
B_HEADS = 16
B_DK = 128
B_DV = 128
CONV_W = 4
B_QK_WIDTH = B_HEADS * B_DK
B_WIDTH = B_HEADS * B_DV
B_CONV_CH = 2 * B_QK_WIDTH + B_WIDTH
N_MEM = 256
M_HEADS = 4
M_HEAD_DIM = 128
M_WIDTH = M_HEADS * M_HEAD_DIM
P_HEADS = 8
N_KEYS = 128
N_EXPERTS = N_KEYS * N_KEYS
P_DKEY = 256
P_HALF = P_DKEY // 2
P_TOPK = 16

LANES = 128
SUBLANES = 8

Z_QKV = 0
Z_Z = Z_QKV + B_CONV_CH
Z_AQ = Z_Z + B_WIDTH
Z_AK = Z_AQ + A_WIDTH
Z_AV = Z_AK + A_KV_WIDTH
Z_AB = Z_AV + A_KV_WIDTH
Z_USED = Z_AB + 2 * B_HEADS
Z_WIDTH = 11264

V7X_VMEM_BYTES = 64 * 1024 * 1024
VMEM_LIMIT = V7X_VMEM_BYTES - 8 * 1024 * 1024

NT_DIMS = (((1,), (1,)), ((), ()))
TN_DIMS = (((0,), (0,)), ((), ()))


def _cparams(sem):
    return pltpu.CompilerParams(dimension_semantics=sem, vmem_limit_bytes=VMEM_LIMIT)


def _rms_rows(x, g):
    ms = jnp.mean(x * x, axis=-1, keepdims=True)
    return x * lax.rsqrt(ms + EPS) * g


def _norm_matmul_kernel(x_ref, g_ref, w_ref, o_ref, *rest, emit_hn):
    hn_ref = rest[-1]
    j = pl.program_id(1)

    @pl.when(j == 0)
    def _():
        hn = _rms_rows(x_ref[...], g_ref[...]).astype(BF16)
        hn_ref[...] = hn
        if emit_hn:
            rest[0][...] = hn

    o_ref[...] = jnp.dot(hn_ref[...], w_ref[...], preferred_element_type=F32)


def norm_matmul(x, g, w, *, tm, tn, emit_hn=False):
    t, d = x.shape
    n = w.shape[1]
    out_shape = [jax.ShapeDtypeStruct((t, n), F32)]
    out_specs = [pl.BlockSpec((tm, tn), lambda i, j: (i, j))]
    if emit_hn:
        out_shape.append(jax.ShapeDtypeStruct((t, d), BF16))
        out_specs.append(pl.BlockSpec((tm, d), lambda i, j: (i, 0)))
    res = pl.pallas_call(
        functools.partial(_norm_matmul_kernel, emit_hn=emit_hn),
        grid=(t // tm, n // tn),
        in_specs=[pl.BlockSpec((tm, d), lambda i, j: (i, 0)),
                  pl.BlockSpec((1, d), lambda i, j: (0, 0)),
                  pl.BlockSpec((d, tn), lambda i, j: (0, j))],
        out_specs=out_specs,
        out_shape=out_shape,
        scratch_shapes=[pltpu.VMEM((tm, d), BF16)],
        compiler_params=_cparams(("parallel", "arbitrary")),
        name="norm_matmul",
    )(x, g.reshape(1, d), w)
    return res if emit_hn else res[0]


def _mix_out_kernel(x_ref, a_ref, b_ref, wa_ref, wb_ref, o_ref):
    acc = jnp.dot(a_ref[...], wa_ref[...], preferred_element_type=F32)
    acc = acc + jnp.dot(b_ref[...], wb_ref[...], preferred_element_type=F32)
    o_ref[...] = x_ref[...] + acc


def mix_out_residual(x, a, b, w, *, tm, tn):
    t, n = x.shape
    ka, kb = a.shape[1], b.shape[1]
    assert ka == kb and w.shape[0] == ka + kb
    return pl.pallas_call(
        _mix_out_kernel,
        grid=(t // tm, n // tn),
        in_specs=[pl.BlockSpec((tm, tn), lambda i, j: (i, j)),
                  pl.BlockSpec((tm, ka), lambda i, j: (i, 0)),
                  pl.BlockSpec((tm, kb), lambda i, j: (i, 0)),
                  pl.BlockSpec((ka, tn), lambda i, j: (0, j)),
                  pl.BlockSpec((kb, tn), lambda i, j: (1, j))],
        out_specs=pl.BlockSpec((tm, tn), lambda i, j: (i, j)),
        out_shape=jax.ShapeDtypeStruct((t, n), F32),
        compiler_params=_cparams(("parallel", "arbitrary")),
        name="mix_out_residual",
    )(x, a, b, w, w)


def _rmsnorm_kernel(x_ref, g_ref, o_ref):
    o_ref[...] = _rms_rows(x_ref[...], g_ref[...])


def rmsnorm_rows(x, g, *, tm):
    t, d = x.shape
    return pl.pallas_call(
        _rmsnorm_kernel,
        grid=(t // tm,),
        in_specs=[pl.BlockSpec((tm, d), lambda i: (i, 0)),
                  pl.BlockSpec((1, d), lambda i: (0, 0))],
        out_specs=pl.BlockSpec((tm, d), lambda i: (i, 0)),
        out_shape=jax.ShapeDtypeStruct((t, d), F32),
        compiler_params=_cparams(("parallel",)),
        name="final_rmsnorm",
    )(x, g.reshape(1, d))


ALIBI_SLOPES = [float(2.0 ** (-8.0 * (n + 1) / A_HEADS)) for n in range(A_HEADS)]
BAND = (WIN_CHUNKS + 1) * CHUNK


def _swa_kernel(sink_ref, q_ref, k0_ref, k1_ref, k2_ref, v0_ref, v1_ref, v2_ref, o_ref, *,
                chunks_per_seq):
    qi = lax.broadcasted_iota(jnp.int32, (CHUNK, BAND), 0)
    kj = lax.broadcasted_iota(jnp.int32, (CHUNK, BAND), 1)
    dist = jnp.abs(qi + WINDOW - kj).astype(F32)
    if chunks_per_seq is None:
        valid = None
    else:
        c = pl.program_id(0) % chunks_per_seq
        valid = kj >= (WIN_CHUNKS - jnp.minimum(c, WIN_CHUNKS)) * CHUNK
    scale = A_HEAD_DIM ** -0.5
    for h in range(A_KV_HEADS):
        hs = slice(h * A_HEAD_DIM, (h + 1) * A_HEAD_DIM)
        qh = jnp.concatenate(
            [q_ref[:, (h * A_GROUP + g) * A_HEAD_DIM:(h * A_GROUP + g + 1) * A_HEAD_DIM]
             for g in range(A_GROUP)], axis=0).astype(BF16)
        kh = jnp.concatenate([k0_ref[:, hs], k1_ref[:, hs], k2_ref[:, hs]], axis=0).astype(BF16)
        vh = jnp.concatenate([v0_ref[:, hs], v1_ref[:, hs], v2_ref[:, hs]], axis=0).astype(BF16)
        s = lax.dot_general(qh, kh, NT_DIMS, preferred_element_type=F32) * scale
        ps = []
        for g in range(A_GROUP):
            n = h * A_GROUP + g
            sg = s[g * CHUNK:(g + 1) * CHUNK] - ALIBI_SLOPES[n] * dist
            if valid is not None:
                sg = jnp.where(valid, sg, -jnp.inf)
            sk = sink_ref[n]
            m = jnp.maximum(jnp.max(sg, axis=-1, keepdims=True), sk)
            p = jnp.exp(sg - m)
            p = p / (jnp.sum(p, axis=-1, keepdims=True) + jnp.exp(sk - m))
            ps.append(p.astype(BF16))
        oh = jnp.dot(jnp.concatenate(ps, axis=0), vh, preferred_element_type=F32)
        for g in range(A_GROUP):
            n = h * A_GROUP + g
            o_ref[:, n * A_HEAD_DIM:(n + 1) * A_HEAD_DIM] = oh[g * CHUNK:(g + 1) * CHUNK].astype(BF16)


def swa_prompt(zz, sinks, n_seq, chunks_per_seq, total_rows):
    qb, kb, vb = Z_AQ // A_WIDTH, Z_AK // A_KV_WIDTH, Z_AV // A_KV_WIDTH

    def hist(j, col):
        def index(i):
            c = i % chunks_per_seq
            return (i - jnp.minimum(c, WIN_CHUNKS - j), col)
        return pl.BlockSpec((CHUNK, A_KV_WIDTH), index)

    return pl.pallas_call(
        functools.partial(_swa_kernel, chunks_per_seq=chunks_per_seq),
        grid=(n_seq * chunks_per_seq,),
        in_specs=[pl.BlockSpec(memory_space=pltpu.SMEM),
                  pl.BlockSpec((CHUNK, A_WIDTH), lambda i: (i, qb)),
                  hist(0, kb), hist(1, kb), hist(2, kb),
                  hist(0, vb), hist(1, vb), hist(2, vb)],
        out_specs=pl.BlockSpec((CHUNK, A_WIDTH), lambda i: (i, 0)),
        out_shape=jax.ShapeDtypeStruct((total_rows, A_WIDTH), BF16),
        compiler_params=_cparams(("parallel",)),
        name="swa_prompt",
    )(sinks, zz, zz, zz, zz, zz, zz, zz)


def swa_sample(a_out, zz, sinks, ck, cv, row_chunk0):
    n_seq = ck.shape[0] // WINDOW
    qb, kb, vb = Z_AQ // A_WIDTH, Z_AK // A_KV_WIDTH, Z_AV // A_KV_WIDTH
    cache0 = pl.BlockSpec((CHUNK, A_KV_WIDTH), lambda i: (WIN_CHUNKS * i, 0))
    cache1 = pl.BlockSpec((CHUNK, A_KV_WIDTH), lambda i: (WIN_CHUNKS * i + 1, 0))

    def body(sink_ref, q_ref, k0, k1, k2, v0, v1, v2, prev_ref, o_ref):
        del prev_ref
        _swa_kernel(sink_ref, q_ref, k0, k1, k2, v0, v1, v2, o_ref, chunks_per_seq=None)

    return pl.pallas_call(
        body,
        grid=(n_seq,),
        in_specs=[pl.BlockSpec(memory_space=pltpu.SMEM),
                  pl.BlockSpec((CHUNK, A_WIDTH), lambda i: (row_chunk0 + i, qb)),
                  cache0, cache1, pl.BlockSpec((CHUNK, A_KV_WIDTH), lambda i: (row_chunk0 + i, kb)),
                  cache0, cache1, pl.BlockSpec((CHUNK, A_KV_WIDTH), lambda i: (row_chunk0 + i, vb)),
                  pl.BlockSpec(memory_space=pl.ANY)],
        out_specs=pl.BlockSpec((CHUNK, A_WIDTH), lambda i: (row_chunk0 + i, 0)),
        out_shape=jax.ShapeDtypeStruct(a_out.shape, BF16),
        input_output_aliases={8: 0},
        compiler_params=_cparams(("parallel",)),
        name="swa_sample",
    )(sinks, zz, ck, ck, zz, cv, cv, zz, a_out)


CONV_ROWS = SUBLANES


def _silu(x):
    return x * jax.nn.sigmoid(x)


def _gdn_kernel(qkv_ref, z_ref, ab_ref, conv0_ref, s0_ref, cw_ref, alog_ref, dtb_ref, ng_ref,
                o_ref, convst_ref, s_ref, xext_ref, *, n_multi, chunks_per_seq):
    i = pl.program_id(0)
    in_multi = i < n_multi
    c = i % chunks_per_seq
    first = jnp.logical_or(jnp.logical_not(in_multi), c == 0)
    last = jnp.logical_or(jnp.logical_not(in_multi), c == chunks_per_seq - 1)

    @pl.when(first)
    def _():
        xext_ref[0:CONV_ROWS, :] = conv0_ref[0]
        s_ref[...] = s0_ref[...]

    xext_ref[CONV_ROWS:CONV_ROWS + CHUNK, :] = qkv_ref[...]

    ri = lax.broadcasted_iota(jnp.int32, (CHUNK, CHUNK), 0)
    ci = lax.broadcasted_iota(jnp.int32, (CHUNK, CHUNK), 1)
    tri_incl = ri >= ci
    tri_strict = ri > ci

    ab = ab_ref[...]
    x = ab + dtb_ref[...]
    softplus = jnp.maximum(x, 0.0) + jnp.log1p(jnp.exp(-jnp.abs(x)))
    gate = -jnp.exp(alog_ref[...]) * softplus
    beta_all = jax.nn.sigmoid(ab)
    gcum = jnp.dot(tri_incl.astype(F32), gate, preferred_element_type=F32,
                   precision=lax.Precision.HIGHEST)
    gcum_t = gcum.T
    egc = jnp.exp(gcum)
    glast = gcum[CHUNK - 1:CHUNK, :]
    etail = jnp.exp(glast - gcum)
    eglast = jnp.exp(glast)

    def conv(col0):
        cs = slice(col0, col0 + LANES)
        y = xext_ref[CONV_ROWS:CONV_ROWS + CHUNK, cs] * cw_ref[CONV_W - 1:CONV_W, cs]
        for j in range(1, CONV_W):
            y = y + xext_ref[CONV_ROWS - j:CONV_ROWS - j + CHUNK, cs] * cw_ref[CONV_W - 1 - j:CONV_W - j, cs]
        return _silu(y)

    def l2n(a):
        return a * lax.rsqrt(jnp.sum(a * a, axis=-1, keepdims=True) + EPS)

    for h in range(B_HEADS):
        q = l2n(conv(h * B_DK)) * (B_DK ** -0.5)
        k = l2n(conv(B_QK_WIDTH + h * B_DK))
        v = conv(2 * B_QK_WIDTH + h * B_DV)
        beta = beta_all[:, B_HEADS + h:B_HEADS + h + 1]
        gc_col = gcum[:, h:h + 1]
        gc_row = gcum_t[h:h + 1, :]
        decay = jnp.exp(jnp.where(tri_incl, gc_col - gc_row, -jnp.inf))
        kbeta = k * beta
        kk_qk = lax.dot_general(jnp.concatenate([kbeta, q], axis=0).astype(BF16), k.astype(BF16),
                                NT_DIMS, preferred_element_type=F32)
        a_mat = jnp.where(tri_strict, kk_qk[:CHUNK] * decay, 0.0)
        attn = kk_qk[CHUNK:] * decay
        xs = jnp.concatenate([v * beta, kbeta * egc[:, h:h + 1]], axis=1)
        m = -a_mat
        for _ in range(6):
            mb = m.astype(BF16)
            y = jnp.dot(mb, jnp.concatenate([xs.astype(BF16), mb], axis=1), preferred_element_type=F32)
            xs = xs + y[:, :2 * B_DV]
            m = y[:, 2 * B_DV:]
        u, w = xs[:, :B_DV], xs[:, B_DV:]
        s_h = s_ref[0, h]
        ws_qs = jnp.dot(jnp.concatenate([w, q * egc[:, h:h + 1]], axis=0).astype(BF16),
                        s_h.astype(BF16), preferred_element_type=F32)
        v_new = u - ws_qs[:CHUNK]
        v_new_b = v_new.astype(BF16)
        o = ws_qs[CHUNK:] + jnp.dot(attn.astype(BF16), v_new_b, preferred_element_type=F32)
        kd = (k * etail[:, h:h + 1]).astype(BF16)
        s_ref[0, h] = s_h * eglast[:, h:h + 1] + lax.dot_general(kd, v_new_b, TN_DIMS,
                                                                 preferred_element_type=F32)
        zs = slice(h * B_DV, (h + 1) * B_DV)
        o_ref[:, zs] = (_rms_rows(o, ng_ref[...]) * _silu(z_ref[:, zs])).astype(BF16)

    xext_ref[0:CONV_ROWS, :] = xext_ref[CHUNK:CHUNK + CONV_ROWS, :]

    @pl.when(last)
    def _():
        convst_ref[0] = xext_ref[0:CONV_ROWS, :]


def gdn_mixer(zz, conv0, s0, conv_w, a_log, dt_bias, norm_g, *, n_multi_seq, chunks_per_seq):
    t = zz.shape[0]
    n_seq = s0.shape[0]
    n_multi = n_multi_seq * chunks_per_seq
    n_steps = t // CHUNK
    assert n_steps == n_multi + (n_seq - n_multi_seq)

    def seq(i):
        return jnp.where(i < n_multi, i // chunks_per_seq, i - n_multi + n_multi_seq)

    pad = lambda a: jnp.pad(a.astype(F32), (0, LANES - a.shape[0])).reshape(1, LANES)
    kern = functools.partial(_gdn_kernel, n_multi=n_multi, chunks_per_seq=chunks_per_seq)
    return pl.pallas_call(
        kern,
        grid=(n_steps,),
        in_specs=[pl.BlockSpec((CHUNK, B_CONV_CH), lambda i: (i, Z_QKV // B_CONV_CH)),
                  pl.BlockSpec((CHUNK, B_WIDTH), lambda i: (i, Z_Z // B_WIDTH)),
                  pl.BlockSpec((CHUNK, LANES), lambda i: (i, Z_AB // LANES)),
                  pl.BlockSpec((1, CONV_ROWS, B_CONV_CH), lambda i: (seq(i), 0, 0)),
                  pl.BlockSpec((1, B_HEADS, B_DK, B_DV), lambda i: (seq(i), 0, 0, 0)),
                  pl.BlockSpec((CONV_W, B_CONV_CH), lambda i: (0, 0)),
                  pl.BlockSpec((1, LANES), lambda i: (0, 0)),
                  pl.BlockSpec((1, LANES), lambda i: (0, 0)),
                  pl.BlockSpec((1, B_DV), lambda i: (0, 0))],
        out_specs=[pl.BlockSpec((CHUNK, B_WIDTH), lambda i: (i, 0)),
                   pl.BlockSpec((1, CONV_ROWS, B_CONV_CH), lambda i: (seq(i), 0, 0)),
                   pl.BlockSpec((1, B_HEADS, B_DK, B_DV), lambda i: (seq(i), 0, 0, 0))],
        out_shape=[jax.ShapeDtypeStruct((t, B_WIDTH), BF16),
                   jax.ShapeDtypeStruct((n_seq, CONV_ROWS, B_CONV_CH), F32),
                   jax.ShapeDtypeStruct((n_seq, B_HEADS, B_DK, B_DV), F32)],
        scratch_shapes=[pltpu.VMEM((CONV_ROWS + CHUNK, B_CONV_CH), F32)],
        compiler_params=_cparams(("arbitrary",)),
        name="gdn_mixer",
    )(zz, zz, zz, conv0, s0, conv_w, pad(a_log), pad(dt_bias), norm_g.reshape(1, B_DV))


def _cross_kernel(x_ref, g_ref, wq_ref, mk_ref, mv_ref, wo_ref, o_ref, *, n_batch):
    x = x_ref[...]
    tm = x.shape[0]
    rows = tm // n_batch
    hn = _rms_rows(x, g_ref[...]).astype(BF16)
    q = jnp.dot(hn, wq_ref[...], preferred_element_type=F32)
    scale = M_HEAD_DIM ** -0.5
    outs = []
    for b in range(n_batch):
        heads = []
        for h in range(M_HEADS):
            hs = slice(h * M_HEAD_DIM, (h + 1) * M_HEAD_DIM)
            qh = q[b * rows:(b + 1) * rows, hs].astype(BF16)
            kh = mk_ref[b * N_MEM:(b + 1) * N_MEM, hs].astype(BF16)
            vh = mv_ref[b * N_MEM:(b + 1) * N_MEM, hs].astype(BF16)
            s = lax.dot_general(qh, kh, NT_DIMS, preferred_element_type=F32) * scale
            p = jnp.exp(s - jnp.max(s, axis=-1, keepdims=True))
            p = p / jnp.sum(p, axis=-1, keepdims=True)
            heads.append(jnp.dot(p.astype(BF16), vh, preferred_element_type=F32))
        outs.append(jnp.concatenate(heads, axis=1))
    o = jnp.concatenate(outs, axis=0) if n_batch > 1 else outs[0]
    o_ref[...] = x + jnp.dot(o.astype(BF16), wo_ref[...], preferred_element_type=F32)


def cross_block(x, g, wq, mk, mv, mk_col, mv_col, wo, *, row0, n_rows, rows_per_batch, tm):
    t, d = x.shape
    if rows_per_batch >= tm:
        n_batch = 1
        tiles_per_batch = rows_per_batch // tm
        mem_index = lambda col: (lambda i: (i // tiles_per_batch, col))
    else:
        n_batch = tm // rows_per_batch
        mem_index = lambda col: (lambda i: (i, col))
    r0 = row0 // tm
    return pl.pallas_call(
        functools.partial(_cross_kernel, n_batch=n_batch),
        grid=(n_rows // tm,),
        in_specs=[pl.BlockSpec((tm, d), lambda i: (r0 + i, 0)),
                  pl.BlockSpec((1, d), lambda i: (0, 0)),
                  pl.BlockSpec((d, M_WIDTH), lambda i: (0, 0)),
                  pl.BlockSpec((n_batch * N_MEM, M_WIDTH), mem_index(mk_col)),
                  pl.BlockSpec((n_batch * N_MEM, M_WIDTH), mem_index(mv_col)),
                  pl.BlockSpec((M_WIDTH, d), lambda i: (0, 0))],
        out_specs=pl.BlockSpec((tm, d), lambda i: (r0 + i, 0)),
        out_shape=jax.ShapeDtypeStruct((t, d), F32),
        input_output_aliases={0: 0},
        compiler_params=_cparams(("parallel",)),
        name="cross_block",
    )(x, g.reshape(1, d), wq, mk, mv, wo)


N_CAND = 50
CAND_ROWS = 56


def _remove_first(cur, mk, iota, sentinel):
    idx = jnp.min(jnp.where(cur == mk, iota, sentinel), axis=0, keepdims=True)
    return jnp.where(iota == idx, -jnp.inf, cur)


def _peer_select_kernel(q_ref, sk1_ref, sk2_ref, s1_ref, s2_ref, tau_ref, m_ref, iz_ref, cand_ref):
    tm = q_ref.shape[0]
    iota_k = lax.broadcasted_iota(jnp.int32, (N_KEYS, tm), 0)
    iota_c = lax.broadcasted_iota(jnp.int32, (CAND_ROWS, tm), 0)

    def top16(s):
        vals = []
        cur = s
        for k in range(P_TOPK):
            mk = jnp.max(cur, axis=0, keepdims=True)
            vals.append(mk)
            if k + 1 < P_TOPK:
                cur = _remove_first(cur, mk, iota_k, N_KEYS)
        return vals

    cand_ref[...] = jnp.full((CAND_ROWS, tm), -jnp.inf, F32)
    for h in range(P_HEADS):
        q1 = q_ref[:, h * P_DKEY:h * P_DKEY + P_HALF].astype(BF16)
        q2 = q_ref[:, h * P_DKEY + P_HALF:(h + 1) * P_DKEY].astype(BF16)
        s1 = lax.dot_general(sk1_ref[h], q1, NT_DIMS, preferred_element_type=F32)
        s2 = lax.dot_general(sk2_ref[h], q2, NT_DIMS, preferred_element_type=F32)
        s1_ref[h] = s1
        s2_ref[h] = s2
        t1 = top16(s1)
        t2 = top16(s2)
        r = 0
        for a in range(P_TOPK):
            for b in range(P_TOPK):
                if (a + 1) * (b + 1) <= P_TOPK:
                    cand_ref[r:r + 1, :] = t1[a] + t2[b]
                    r += 1
        assert r == N_CAND
        cur = cand_ref[...]
        m = t1[0] + t2[0]
        z = jnp.zeros((1, tm), F32)
        mk = m
        for k in range(P_TOPK):
            mk = jnp.max(cur, axis=0, keepdims=True)
            z = z + jnp.exp(mk - m)
            if k + 1 < P_TOPK:
                cur = _remove_first(cur, mk, iota_c, CAND_ROWS)
        tau_ref[h:h + 1, :] = mk
        m_ref[h:h + 1, :] = m
        iz_ref[h:h + 1, :] = 1.0 / z


def peer_select(q, sk1, sk2, *, tm):
    t = q.shape[0]
    big = jax.ShapeDtypeStruct((P_HEADS, N_KEYS, t), F32)
    small = jax.ShapeDtypeStruct((P_HEADS, t), F32)
    big_spec = pl.BlockSpec((P_HEADS, N_KEYS, tm), lambda i: (0, 0, i))
    small_spec = pl.BlockSpec((P_HEADS, tm), lambda i: (0, i))
    key_spec = pl.BlockSpec((P_HEADS, N_KEYS, P_HALF), lambda i: (0, 0, 0))
    return pl.pallas_call(
        _peer_select_kernel,
        grid=(t // tm,),
        in_specs=[pl.BlockSpec((tm, P_HEADS * P_DKEY), lambda i: (i, 0)), key_spec, key_spec],
        out_specs=[big_spec, big_spec, small_spec, small_spec, small_spec],
        out_shape=[big, big, small, small, small],
        scratch_shapes=[pltpu.VMEM((CAND_ROWS, tm), F32)],
        compiler_params=_cparams(("parallel",)),
        name="peer_select",
    )(q, sk1, sk2)


SQRT_HALF = float(np.sqrt(0.5))


def _peer_main_kernel(hnT_ref, u_ref, vt_ref, s1_ref, s2_ref, tau_ref, m_ref, iz_ref, o_ref,
                      a_ref, p_ref):
    e = pl.program_id(1)
    te, tm = a_ref.shape
    rows = te // N_KEYS

    @pl.when(e == 0)
    def _():
        o_ref[...] = jnp.zeros_like(o_ref)

    a_ref[...] = jnp.dot(u_ref[...], hnT_ref[...], preferred_element_type=F32)

    def row_body(r, carry):
        i1 = e * rows + r
        r0 = pl.multiple_of(r * N_KEYS, N_KEYS)
        for c in range(tm // LANES):
            cs = slice(c * LANES, (c + 1) * LANES)
            g = jnp.zeros((N_KEYS, LANES), F32)
            for h in range(P_HEADS):
                s = s1_ref[i1, h:h + 1, cs] + s2_ref[h, :, cs]
                w = jnp.exp(s - m_ref[h:h + 1, cs]) * iz_ref[h:h + 1, cs]
                g = g + jnp.where(s >= tau_ref[h:h + 1, cs], w, 0.0)
            a = a_ref[pl.ds(r0, N_KEYS), cs]
            act = 0.5 * a * (1.0 + lax.erf(a * SQRT_HALF))
            p_ref[pl.ds(r0, N_KEYS), cs] = (act * g).astype(BF16)
        return carry

    lax.fori_loop(0, rows, row_body, 0)
    o_ref[...] += jnp.dot(vt_ref[...], p_ref[...], preferred_element_type=F32)


def peer_main(hnT, u, vt, s1, s2, tau, m, iz, *, tm, te):
    d, t = hnT.shape
    small_spec = pl.BlockSpec((P_HEADS, tm), lambda i, e: (0, i))
    big_spec = pl.BlockSpec((P_HEADS, N_KEYS, tm), lambda i, e: (0, 0, i))
    s1_spec = pl.BlockSpec((N_KEYS, P_HEADS, tm), lambda i, e: (0, 0, i))
    return pl.pallas_call(
        _peer_main_kernel,
        grid=(t // tm, N_EXPERTS // te),
        in_specs=[pl.BlockSpec((d, tm), lambda i, e: (0, i)),
                  pl.BlockSpec((te, d), lambda i, e: (e, 0)),
                  pl.BlockSpec((d, te), lambda i, e: (0, e)),
                  s1_spec, big_spec, small_spec, small_spec, small_spec],
        out_specs=pl.BlockSpec((d, tm), lambda i, e: (0, i)),
        out_shape=jax.ShapeDtypeStruct((d, t), F32),
        scratch_shapes=[pltpu.VMEM((te, tm), F32), pltpu.VMEM((te, tm), BF16)],
        compiler_params=_cparams(("parallel", "arbitrary")),
        name="peer_main",
    )(hnT, u, vt, jnp.transpose(s1, (1, 0, 2)), s2, tau, m, iz)


def kernel(x_prompt, x_sample, mem_prompt, cache_swa_k, cache_swa_v, state_conv, state_gdn, cache_mem_k, cache_mem_v, ln_mix, w_in, conv_w, a_log, dt_bias, gdn_norm, sinks, w_out, ln_cross, ln_mem, w_mq, w_mk, w_mv, w_mo, ln_ffn, w_pq, sub_keys1, sub_keys2, expert_u, expert_v, ln_final):
    Bp, Sp, D = x_prompt.shape
    Bs, Ts, _ = x_sample.shape
    assert Ts == CHUNK and Sp % CHUNK == 0 and cache_swa_k.shape[2] == WINDOW
    n_p = Bp * Sp
    n_s = Bs * Ts
    n_tok = n_p + n_s
    cps = Sp // CHUNK
    x = jnp.concatenate([x_prompt.reshape(n_p, D), x_sample.reshape(n_s, D)], axis=0)
    mem = mem_prompt.reshape(Bp * N_MEM, D)
    conv_pad = ((0, 0), (CONV_ROWS - (CONV_W - 1), 0), (0, 0))

    outs = [[] for _ in range(10)]
    for l in range(DEPTH):
        wi = w_in[l]
        o_qkv = A_WIDTH + 2 * A_KV_WIDTH
        o_ab = o_qkv + B_CONV_CH
        o_z = o_ab + 2 * B_HEADS
        w_in_l = jnp.concatenate(
            [wi[:, o_qkv:o_ab], wi[:, o_z:], wi[:, :o_qkv], wi[:, o_ab:o_z],
             jnp.zeros((D, Z_WIDTH - Z_USED), F32)], axis=1).astype(BF16)
        w_out_l = w_out[l].astype(BF16)
        w_mq_l = w_mq[l].astype(BF16)
        w_mkv_l = jnp.concatenate([w_mk[l], w_mv[l]], axis=1).astype(BF16)
        w_mo_l = w_mo[l].astype(BF16)
        w_pq_l = w_pq[l].astype(BF16)
        u_l = expert_u[l].astype(BF16)
        vt_l = expert_v[l].astype(BF16).T
        sk1_l = sub_keys1[l].astype(BF16)
        sk2_l = sub_keys2[l].astype(BF16)

        zz = norm_matmul(x, ln_mix[l], w_in_l, tm=512, tn=1024)
        ck = cache_swa_k[l].reshape(Bs * WINDOW, A_KV_WIDTH)
        cv = cache_swa_v[l].reshape(Bs * WINDOW, A_KV_WIDTH)
        a_out = swa_prompt(zz, sinks[l], Bp, cps, n_tok)
        a_out = swa_sample(a_out, zz, sinks[l], ck, cv, n_p // CHUNK)
        conv0 = jnp.concatenate([jnp.zeros((Bp, CONV_ROWS, B_CONV_CH), F32),
                                 jnp.pad(state_conv[l], conv_pad)], axis=0)
        s0 = jnp.concatenate([jnp.zeros((Bp, B_HEADS, B_DK, B_DV), F32), state_gdn[l]], axis=0)
        b_out, convst, s_fin = gdn_mixer(zz, conv0, s0, conv_w[l], a_log[l], dt_bias[l], gdn_norm[l],
                                         n_multi_seq=Bp, chunks_per_seq=cps)
        x = mix_out_residual(x, a_out, b_out, w_out_l, tm=512, tn=1024)

        mkv = norm_matmul(mem, ln_mem[l], w_mkv_l, tm=512, tn=1024)
        x = cross_block(x, ln_cross[l], w_mq_l, mkv, mkv, 0, 1, w_mo_l,
                        row0=0, n_rows=n_p, rows_per_batch=Sp, tm=256)
        cmk = cache_mem_k[l].reshape(Bs * N_MEM, M_WIDTH)
        cmv = cache_mem_v[l].reshape(Bs * N_MEM, M_WIDTH)
        x = cross_block(x, ln_cross[l], w_mq_l, cmk, cmv, 0, 0, w_mo_l,
                        row0=n_p, n_rows=n_s, rows_per_batch=Ts, tm=256)

        qp, hn = norm_matmul(x, ln_ffn[l], w_pq_l, tm=512, tn=512, emit_hn=True)
        s1, s2, tau, m, iz = peer_select(qp, sk1_l, sk2_l, tm=256)
        yT = peer_main(hn.T, u_l, vt_l, s1, s2, tau, m, iz, tm=512, te=512)
        x = x + yT.T

        zp = zz[:n_p].reshape(Bp, Sp, Z_WIDTH)
        zs = zz[n_p:].reshape(Bs, Ts, Z_WIDTH)
        kv_shape = (A_KV_HEADS, A_HEAD_DIM)
        kb_p = zp[:, Sp - WINDOW:, Z_AK:Z_AV].reshape((Bp, WINDOW) + kv_shape)
        vb_p = zp[:, Sp - WINDOW:, Z_AV:Z_AB].reshape((Bp, WINDOW) + kv_shape)
        kb_s = jnp.concatenate([cache_swa_k[l][:, Ts:], zs[..., Z_AK:Z_AV].reshape((Bs, Ts) + kv_shape)], axis=1)
        vb_s = jnp.concatenate([cache_swa_v[l][:, Ts:], zs[..., Z_AV:Z_AB].reshape((Bs, Ts) + kv_shape)], axis=1)
        cst = convst[:, CONV_ROWS - (CONV_W - 1):]
        mk = mkv[:, :M_WIDTH].reshape(Bp, N_MEM, M_HEADS, M_HEAD_DIM)
        mv = mkv[:, M_WIDTH:].reshape(Bp, N_MEM, M_HEADS, M_HEAD_DIM)
        for lst, val in zip(outs, (kb_p, vb_p, cst[:Bp], s_fin[:Bp], mk, mv, kb_s, vb_s, cst[Bp:], s_fin[Bp:])):
            lst.append(val)

    y = rmsnorm_rows(x, ln_final, tm=512)
    y_prompt = y[:n_p].reshape(Bp, Sp, D)
    y_sample = y[n_p:].reshape(Bs, Ts, D)
    return (y_prompt, y_sample) + tuple(jnp.stack(o) for o in outs)
```

```python
import functools

import jax
import jax.numpy as jnp
import numpy as np
from jax import lax
from jax.experimental import pallas as pl
from jax.experimental.pallas import tpu as pltpu

F32 = jnp.float32
BF16 = jnp.bfloat16

D_MODEL = 4096
DEPTH = 4
CHUNK = 64
EPS = 1e-6
A_HEADS = 32
A_KV_HEADS = 4
A_GROUP = A_HEADS // A_KV_HEADS
A_HEAD_DIM = 64
WINDOW = 128
WIN_CHUNKS = WINDOW // CHUNK
A_WIDTH = A_HEADS * A_HEAD_DIM
A_KV_WIDTH = A_KV_HEADS * A_HEAD_DIM
B_HEADS = 16
B_DK = 128
B_DV = 128
CONV_W = 4
B_QK_WIDTH = B_HEADS * B_DK
B_WIDTH = B_HEADS * B_DV
B_CONV_CH = 2 * B_QK_WIDTH + B_WIDTH
N_MEM = 256
M_HEADS = 4
M_HEAD_DIM = 128
M_WIDTH = M_HEADS * M_HEAD_DIM
P_HEADS = 8
N_KEYS = 128
N_EXPERTS = N_KEYS * N_KEYS
P_DKEY = 256
P_HALF = P_DKEY // 2
P_TOPK = 16

LANES = 128
SUBLANES = 8

Z_QKV = 0
Z_Z = Z_QKV + B_CONV_CH
Z_AQ = Z_Z + B_WIDTH
Z_AK = Z_AQ + A_WIDTH
Z_AV = Z_AK + A_KV_WIDTH
Z_AB = Z_AV + A_KV_WIDTH
Z_USED = Z_AB + 2 * B_HEADS
Z_WIDTH = 11264

V7X_VMEM_BYTES = 64 * 1024 * 1024
VMEM_LIMIT = V7X_VMEM_BYTES - 8 * 1024 * 1024

NT_DIMS = (((1,), (1,)), ((), ()))
TN_DIMS = (((0,), (0,)), ((), ()))


def _cparams(sem):
    return pltpu.CompilerParams(dimension_semantics=sem, vmem_limit_bytes=VMEM_LIMIT)


def _rms_rows(x, g):
    ms = jnp.mean(x * x, axis=-1, keepdims=True)
    return x * lax.rsqrt(ms + EPS) * g


def _norm_matmul_kernel(x_ref, g_ref, w_ref, o_ref, *rest, emit_hn):
    hn_ref = rest[-1]
    j = pl.program_id(1)

    @pl.when(j == 0)
    def _():
        hn = _rms_rows(x_ref[...], g_ref[...]).astype(BF16)
        hn_ref[...] = hn
        if emit_hn:
            rest[0][...] = hn.T

    o_ref[...] = jnp.dot(hn_ref[...], w_ref[...], preferred_element_type=F32)


def norm_matmul(x, g, w, *, tm, tn, emit_hn=False):
    t, d = x.shape
    n = w.shape[1]
    out_shape = [jax.ShapeDtypeStruct((t, n), F32)]
    out_specs = [pl.BlockSpec((tm, tn), lambda i, j: (i, j))]
    if emit_hn:
        out_shape.append(jax.ShapeDtypeStruct((d, t), BF16))
        out_specs.append(pl.BlockSpec((d, tm), lambda i, j: (0, i)))
    res = pl.pallas_call(
        functools.partial(_norm_matmul_kernel, emit_hn=emit_hn),
        grid=(t // tm, n // tn),
        in_specs=[pl.BlockSpec((tm, d), lambda i, j: (i, 0)),
                  pl.BlockSpec((1, d), lambda i, j: (0, 0)),
                  pl.BlockSpec((d, tn), lambda i, j: (0, j))],
        out_specs=out_specs,
        out_shape=out_shape,
        scratch_shapes=[pltpu.VMEM((tm, d), BF16)],
        compiler_params=_cparams(("parallel", "arbitrary")),
        name="norm_matmul",
    )(x, g.reshape(1, d), w)
    return res if emit_hn else res[0]


def _mix_out_kernel(x_ref, a_ref, b_ref, wa_ref, wb_ref, o_ref):
    acc = jnp.dot(a_ref[...], wa_ref[...], preferred_element_type=F32)
    acc = acc + jnp.dot(b_ref[...], wb_ref[...], preferred_element_type=F32)
    o_ref[...] = x_ref[...] + acc


def mix_out_residual(x, a, b, w, *, tm, tn):
    t, n = x.shape
    ka, kb = a.shape[1], b.shape[1]
    assert ka == kb and w.shape[0] == ka + kb
    return pl.pallas_call(
        _mix_out_kernel,
        grid=(t // tm, n // tn),
        in_specs=[pl.BlockSpec((tm, tn), lambda i, j: (i, j)),
                  pl.BlockSpec((tm, ka), lambda i, j: (i, 0)),
                  pl.BlockSpec((tm, kb), lambda i, j: (i, 0)),
                  pl.BlockSpec((ka, tn), lambda i, j: (0, j)),
                  pl.BlockSpec((kb, tn), lambda i, j: (1, j))],
        out_specs=pl.BlockSpec((tm, tn), lambda i, j: (i, j)),
        out_shape=jax.ShapeDtypeStruct((t, n), F32),
        compiler_params=_cparams(("parallel", "arbitrary")),
        name="mix_out_residual",
    )(x, a, b, w, w)


def _rmsnorm_kernel(x_ref, g_ref, o_ref):
    o_ref[...] = _rms_rows(x_ref[...], g_ref[...])


def rmsnorm_rows(x, g, *, tm):
    t, d = x.shape
    return pl.pallas_call(
        _rmsnorm_kernel,
        grid=(t // tm,),
        in_specs=[pl.BlockSpec((tm, d), lambda i: (i, 0)),
                  pl.BlockSpec((1, d), lambda i: (0, 0))],
        out_specs=pl.BlockSpec((tm, d), lambda i: (i, 0)),
        out_shape=jax.ShapeDtypeStruct((t, d), F32),
        compiler_params=_cparams(("parallel",)),
        name="final_rmsnorm",
    )(x, g.reshape(1, d))


ALIBI_SLOPES = [float(2.0 ** (-8.0 * (n + 1) / A_HEADS)) for n in range(A_HEADS)]
BAND = (WIN_CHUNKS + 1) * CHUNK


def _swa_kernel(sink_ref, q_ref, k0_ref, k1_ref, k2_ref, v0_ref, v1_ref, v2_ref, o_ref, *,
                chunks_per_seq):
    qi = lax.broadcasted_iota(jnp.int32, (CHUNK, BAND), 0)
    kj = lax.broadcasted_iota(jnp.int32, (CHUNK, BAND), 1)
    dist = jnp.abs(qi + WINDOW - kj).astype(F32)
    if chunks_per_seq is None:
        valid = None
    else:
        c = pl.program_id(0) % chunks_per_seq
        valid = kj >= (WIN_CHUNKS - jnp.minimum(c, WIN_CHUNKS)) * CHUNK
    scale = A_HEAD_DIM ** -0.5
    scores, vals = [], []
    for h in range(A_KV_HEADS):
        hs = slice(h * A_HEAD_DIM, (h + 1) * A_HEAD_DIM)
        qh = jnp.concatenate(
            [q_ref[:, (h * A_GROUP + g) * A_HEAD_DIM:(h * A_GROUP + g + 1) * A_HEAD_DIM]
             for g in range(A_GROUP)], axis=0).astype(BF16)
        kh = jnp.concatenate([k0_ref[:, hs], k1_ref[:, hs], k2_ref[:, hs]], axis=0).astype(BF16)
        vals.append(jnp.concatenate([v0_ref[:, hs], v1_ref[:, hs], v2_ref[:, hs]], axis=0).astype(BF16))
        scores.append(lax.dot_general(qh, kh, NT_DIMS, preferred_element_type=F32) * scale)
    probs = []
    for h in range(A_KV_HEADS):
        ps = []
        for g in range(A_GROUP):
            n = h * A_GROUP + g
            sg = scores[h][g * CHUNK:(g + 1) * CHUNK] - ALIBI_SLOPES[n] * dist
            if valid is not None:
                sg = jnp.where(valid, sg, -jnp.inf)
            sk = sink_ref[n]
            m = jnp.maximum(jnp.max(sg, axis=-1, keepdims=True), sk)
            p = jnp.exp(sg - m)
            p = p / (jnp.sum(p, axis=-1, keepdims=True) + jnp.exp(sk - m))
            ps.append(p.astype(BF16))
        probs.append(jnp.concatenate(ps, axis=0))
    for h in range(A_KV_HEADS):
        oh = jnp.dot(probs[h], vals[h], preferred_element_type=F32)
        for g in range(A_GROUP):
            n = h * A_GROUP + g
            o_ref[:, n * A_HEAD_DIM:(n + 1) * A_HEAD_DIM] = oh[g * CHUNK:(g + 1) * CHUNK].astype(BF16)


def swa_prompt(zz, sinks, n_seq, chunks_per_seq, total_rows):
    qb, kb, vb = Z_AQ // A_WIDTH, Z_AK // A_KV_WIDTH, Z_AV // A_KV_WIDTH

    def hist(j, col):
        def index(i):
            c = i % chunks_per_seq
            return (i - jnp.minimum(c, WIN_CHUNKS - j), col)
        return pl.BlockSpec((CHUNK, A_KV_WIDTH), index)

    return pl.pallas_call(
        functools.partial(_swa_kernel, chunks_per_seq=chunks_per_seq),
        grid=(n_seq * chunks_per_seq,),
        in_specs=[pl.BlockSpec(memory_space=pltpu.SMEM),
                  pl.BlockSpec((CHUNK, A_WIDTH), lambda i: (i, qb)),
                  hist(0, kb), hist(1, kb), hist(2, kb),
                  hist(0, vb), hist(1, vb), hist(2, vb)],
        out_specs=pl.BlockSpec((CHUNK, A_WIDTH), lambda i: (i, 0)),
        out_shape=jax.ShapeDtypeStruct((total_rows, A_WIDTH), BF16),
        compiler_params=_cparams(("parallel",)),
        name="swa_prompt",
    )(sinks, zz, zz, zz, zz, zz, zz, zz)


def swa_sample(a_out, zz, sinks, ck, cv, row_chunk0):
    n_seq = ck.shape[0] // WINDOW
    qb, kb, vb = Z_AQ // A_WIDTH, Z_AK // A_KV_WIDTH, Z_AV // A_KV_WIDTH
    cache0 = pl.BlockSpec((CHUNK, A_KV_WIDTH), lambda i: (WIN_CHUNKS * i, 0))
    cache1 = pl.BlockSpec((CHUNK, A_KV_WIDTH), lambda i: (WIN_CHUNKS * i + 1, 0))

    def body(sink_ref, q_ref, k0, k1, k2, v0, v1, v2, prev_ref, o_ref):
        del prev_ref
        _swa_kernel(sink_ref, q_ref, k0, k1, k2, v0, v1, v2, o_ref, chunks_per_seq=None)

    return pl.pallas_call(
        body,
        grid=(n_seq,),
        in_specs=[pl.BlockSpec(memory_space=pltpu.SMEM),
                  pl.BlockSpec((CHUNK, A_WIDTH), lambda i: (row_chunk0 + i, qb)),
                  cache0, cache1, pl.BlockSpec((CHUNK, A_KV_WIDTH), lambda i: (row_chunk0 + i, kb)),
                  cache0, cache1, pl.BlockSpec((CHUNK, A_KV_WIDTH), lambda i: (row_chunk0 + i, vb)),
                  pl.BlockSpec(memory_space=pl.ANY)],
        out_specs=pl.BlockSpec((CHUNK, A_WIDTH), lambda i: (row_chunk0 + i, 0)),
        out_shape=jax.ShapeDtypeStruct(a_out.shape, BF16),
        input_output_aliases={8: 0},
        compiler_params=_cparams(("parallel",)),
        name="swa_sample",
    )(sinks, zz, ck, ck, zz, cv, cv, zz, a_out)


CONV_ROWS = SUBLANES


def _silu(x):
    return x * jax.nn.sigmoid(x)


def _gdn_kernel(qkv_ref, z_ref, ab_ref, conv0_ref, s0_ref, cw_ref, alog_ref, dtb_ref, ng_ref,
                o_ref, convst_ref, s_ref, xext_ref, *, n_multi, chunks_per_seq):
    i = pl.program_id(0)
    in_multi = i < n_multi
    c = i % chunks_per_seq
    first = jnp.logical_or(jnp.logical_not(in_multi), c == 0)
    last = jnp.logical_or(jnp.logical_not(in_multi), c == chunks_per_seq - 1)

    @pl.when(first)
    def _():
        xext_ref[0:CONV_ROWS, :] = conv0_ref[0]
        s_ref[...] = s0_ref[...]

    xext_ref[CONV_ROWS:CONV_ROWS + CHUNK, :] = qkv_ref[...]

    ri = lax.broadcasted_iota(jnp.int32, (CHUNK, CHUNK), 0)
    ci = lax.broadcasted_iota(jnp.int32, (CHUNK, CHUNK), 1)
    tri_incl = ri >= ci
    tri_strict = ri > ci

    ab = ab_ref[...]
    x = ab + dtb_ref[...]
    softplus = jnp.maximum(x, 0.0) + jnp.log1p(jnp.exp(-jnp.abs(x)))
    gate = -jnp.exp(alog_ref[...]) * softplus
    beta_all = jax.nn.sigmoid(ab)
    gcum = jnp.dot(tri_incl.astype(F32), gate, preferred_element_type=F32,
                   precision=lax.Precision.HIGHEST)
    gcum_t = gcum.T
    egc = jnp.exp(gcum)
    glast = gcum[CHUNK - 1:CHUNK, :]
    etail = jnp.exp(glast - gcum)
    eglast = jnp.exp(glast)

    def conv(col0):
        cs = slice(col0, col0 + LANES)
        y = xext_ref[CONV_ROWS:CONV_ROWS + CHUNK, cs] * cw_ref[CONV_W - 1:CONV_W, cs]
        for j in range(1, CONV_W):
            y = y + xext_ref[CONV_ROWS - j:CONV_ROWS - j + CHUNK, cs] * cw_ref[CONV_W - 1 - j:CONV_W - j, cs]
        return _silu(y)

    def l2n(a):
        return a * lax.rsqrt(jnp.sum(a * a, axis=-1, keepdims=True) + EPS)

    heads = range(B_HEADS)
    qs, ks, attns, xss, ms = [], [], [], [], []
    for h in heads:
        q = l2n(conv(h * B_DK)) * (B_DK ** -0.5)
        k = l2n(conv(B_QK_WIDTH + h * B_DK))
        v = conv(2 * B_QK_WIDTH + h * B_DV)
        beta = beta_all[:, B_HEADS + h:B_HEADS + h + 1]
        gc_col = gcum[:, h:h + 1]
        gc_row = gcum_t[h:h + 1, :]
        decay = jnp.exp(jnp.where(tri_incl, gc_col - gc_row, -jnp.inf))
        kbeta = k * beta
        kk_qk = lax.dot_general(jnp.concatenate([kbeta, q], axis=0).astype(BF16), k.astype(BF16),
                                NT_DIMS, preferred_element_type=F32)
        ms.append(-jnp.where(tri_strict, kk_qk[:CHUNK] * decay, 0.0))
        attns.append((kk_qk[CHUNK:] * decay).astype(BF16))
        xss.append(jnp.concatenate([v * beta, kbeta * egc[:, h:h + 1]], axis=1))
        qs.append((q * egc[:, h:h + 1]).astype(BF16))
        ks.append((k * etail[:, h:h + 1]).astype(BF16))
    for _ in range(6):
        for h in heads:
            mb = ms[h].astype(BF16)
            y = jnp.dot(mb, jnp.concatenate([xss[h].astype(BF16), mb], axis=1), preferred_element_type=F32)
            xss[h] = xss[h] + y[:, :2 * B_DV]
            ms[h] = y[:, 2 * B_DV:]
    v_news, os_ = [], []
    for h in heads:
        u, w = xss[h][:, :B_DV], xss[h][:, B_DV:]
        ws_qs = jnp.dot(jnp.concatenate([w.astype(BF16), qs[h]], axis=0),
                        s_ref[0, h].astype(BF16), preferred_element_type=F32)
        v_news.append((u - ws_qs[:CHUNK]).astype(BF16))
        os_.append(ws_qs[CHUNK:])
    for h in heads:
        o = os_[h] + jnp.dot(attns[h], v_news[h], preferred_element_type=F32)
        s_ref[0, h] = s_ref[0, h] * eglast[:, h:h + 1] + lax.dot_general(
            ks[h], v_news[h], TN_DIMS, preferred_element_type=F32)
        zs = slice(h * B_DV, (h + 1) * B_DV)
        o_ref[:, zs] = (_rms_rows(o, ng_ref[...]) * _silu(z_ref[:, zs])).astype(BF16)

    xext_ref[0:CONV_ROWS, :] = xext_ref[CHUNK:CHUNK + CONV_ROWS, :]

    @pl.when(last)
    def _():
        convst_ref[0] = xext_ref[0:CONV_ROWS, :]


def gdn_mixer(zz, conv0, s0, conv_w, a_log, dt_bias, norm_g, *, n_multi_seq, chunks_per_seq):
    t = zz.shape[0]
    n_seq = s0.shape[0]
    n_multi = n_multi_seq * chunks_per_seq
    n_steps = t // CHUNK
    assert n_steps == n_multi + (n_seq - n_multi_seq)

    def seq(i):
        return jnp.where(i < n_multi, i // chunks_per_seq, i - n_multi + n_multi_seq)

    pad = lambda a: jnp.pad(a.astype(F32), (0, LANES - a.shape[0])).reshape(1, LANES)
    kern = functools.partial(_gdn_kernel, n_multi=n_multi, chunks_per_seq=chunks_per_seq)
    return pl.pallas_call(
        kern,
        grid=(n_steps,),
        in_specs=[pl.BlockSpec((CHUNK, B_CONV_CH), lambda i: (i, Z_QKV // B_CONV_CH)),
                  pl.BlockSpec((CHUNK, B_WIDTH), lambda i: (i, Z_Z // B_WIDTH)),
                  pl.BlockSpec((CHUNK, LANES), lambda i: (i, Z_AB // LANES)),
                  pl.BlockSpec((1, CONV_ROWS, B_CONV_CH), lambda i: (seq(i), 0, 0)),
                  pl.BlockSpec((1, B_HEADS, B_DK, B_DV), lambda i: (seq(i), 0, 0, 0)),
                  pl.BlockSpec((CONV_W, B_CONV_CH), lambda i: (0, 0)),
                  pl.BlockSpec((1, LANES), lambda i: (0, 0)),
                  pl.BlockSpec((1, LANES), lambda i: (0, 0)),
                  pl.BlockSpec((1, B_DV), lambda i: (0, 0))],
        out_specs=[pl.BlockSpec((CHUNK, B_WIDTH), lambda i: (i, 0)),
                   pl.BlockSpec((1, CONV_ROWS, B_CONV_CH), lambda i: (seq(i), 0, 0)),
                   pl.BlockSpec((1, B_HEADS, B_DK, B_DV), lambda i: (seq(i), 0, 0, 0))],
        out_shape=[jax.ShapeDtypeStruct((t, B_WIDTH), BF16),
                   jax.ShapeDtypeStruct((n_seq, CONV_ROWS, B_CONV_CH), F32),
                   jax.ShapeDtypeStruct((n_seq, B_HEADS, B_DK, B_DV), F32)],
        scratch_shapes=[pltpu.VMEM((CONV_ROWS + CHUNK, B_CONV_CH), F32)],
        compiler_params=_cparams(("arbitrary",)),
        name="gdn_mixer",
    )(zz, zz, zz, conv0, s0, conv_w, pad(a_log), pad(dt_bias), norm_g.reshape(1, B_DV))


def _cross_kernel(x_ref, g_ref, wq_ref, mk_ref, mv_ref, wo_ref, o_ref, *, n_batch):
    x = x_ref[...]
    tm = x.shape[0]
    rows = tm // n_batch
    hn = _rms_rows(x, g_ref[...]).astype(BF16)
    q = jnp.dot(hn, wq_ref[...], preferred_element_type=F32)
    scale = M_HEAD_DIM ** -0.5
    pairs = [(b, h) for b in range(n_batch) for h in range(M_HEADS)]
    scores, vals = {}, {}
    for b, h in pairs:
        hs = slice(h * M_HEAD_DIM, (h + 1) * M_HEAD_DIM)
        qh = q[b * rows:(b + 1) * rows, hs].astype(BF16)
        kh = mk_ref[b * N_MEM:(b + 1) * N_MEM, hs].astype(BF16)
        vals[b, h] = mv_ref[b * N_MEM:(b + 1) * N_MEM, hs].astype(BF16)
        scores[b, h] = lax.dot_general(qh, kh, NT_DIMS, preferred_element_type=F32) * scale
    probs = {}
    for b, h in pairs:
        s = scores[b, h]
        p = jnp.exp(s - jnp.max(s, axis=-1, keepdims=True))
        probs[b, h] = (p / jnp.sum(p, axis=-1, keepdims=True)).astype(BF16)
    outs = []
    for b in range(n_batch):
        outs.append(jnp.concatenate(
            [jnp.dot(probs[b, h], vals[b, h], preferred_element_type=F32) for h in range(M_HEADS)], axis=1))
    o = jnp.concatenate(outs, axis=0) if n_batch > 1 else outs[0]
    o_ref[...] = x + jnp.dot(o.astype(BF16), wo_ref[...], preferred_element_type=F32)


def cross_block(x, g, wq, mk, mv, mk_col, mv_col, wo, *, row0, n_rows, rows_per_batch, tm):
    t, d = x.shape
    if rows_per_batch >= tm:
        n_batch = 1
        tiles_per_batch = rows_per_batch // tm
        mem_index = lambda col: (lambda i: (i // tiles_per_batch, col))
    else:
        n_batch = tm // rows_per_batch
        mem_index = lambda col: (lambda i: (i, col))
    r0 = row0 // tm
    return pl.pallas_call(
        functools.partial(_cross_kernel, n_batch=n_batch),
        grid=(n_rows // tm,),
        in_specs=[pl.BlockSpec((tm, d), lambda i: (r0 + i, 0)),
                  pl.BlockSpec((1, d), lambda i: (0, 0)),
                  pl.BlockSpec((d, M_WIDTH), lambda i: (0, 0)),
                  pl.BlockSpec((n_batch * N_MEM, M_WIDTH), mem_index(mk_col)),
                  pl.BlockSpec((n_batch * N_MEM, M_WIDTH), mem_index(mv_col)),
                  pl.BlockSpec((M_WIDTH, d), lambda i: (0, 0))],
        out_specs=pl.BlockSpec((tm, d), lambda i: (r0 + i, 0)),
        out_shape=jax.ShapeDtypeStruct((t, d), F32),
        input_output_aliases={0: 0},
        compiler_params=_cparams(("parallel",)),
        name="cross_block",
    )(x, g.reshape(1, d), wq, mk, mv, wo)


N_CAND = 50
CAND_ROWS = 56


def _remove_first(cur, mk, iota, sentinel):
    idx = jnp.min(jnp.where(cur == mk, iota, sentinel), axis=0, keepdims=True)
    return jnp.where(iota == idx, -jnp.inf, cur)


def _peer_select_kernel(q_ref, sk1_ref, sk2_ref, s1_ref, s2_ref, tau_ref, m_ref, iz_ref, cand_ref):
    tm = q_ref.shape[0]
    iota_k = lax.broadcasted_iota(jnp.int32, (N_KEYS, tm), 0)
    iota_c = lax.broadcasted_iota(jnp.int32, (CAND_ROWS, tm), 0)

    def top16(s):
        vals = []
        cur = s
        for k in range(P_TOPK):
            mk = jnp.max(cur, axis=0, keepdims=True)
            vals.append(mk)
            if k + 1 < P_TOPK:
                cur = _remove_first(cur, mk, iota_k, N_KEYS)
        return vals

    cand_ref[...] = jnp.full((CAND_ROWS, tm), -jnp.inf, F32)
    for h in range(P_HEADS):
        q1 = q_ref[:, h * P_DKEY:h * P_DKEY + P_HALF].astype(BF16)
        q2 = q_ref[:, h * P_DKEY + P_HALF:(h + 1) * P_DKEY].astype(BF16)
        s1 = lax.dot_general(sk1_ref[h], q1, NT_DIMS, preferred_element_type=F32)
        s2 = lax.dot_general(sk2_ref[h], q2, NT_DIMS, preferred_element_type=F32)
        s1_ref[h] = s1
        s2_ref[h] = s2
        t1 = top16(s1)
        t2 = top16(s2)
        r = 0
        for a in range(P_TOPK):
            for b in range(P_TOPK):
                if (a + 1) * (b + 1) <= P_TOPK:
                    cand_ref[r:r + 1, :] = t1[a] + t2[b]
                    r += 1
        assert r == N_CAND
        cur = cand_ref[...]
        m = t1[0] + t2[0]
        z = jnp.zeros((1, tm), F32)
        mk = m
        for k in range(P_TOPK):
            mk = jnp.max(cur, axis=0, keepdims=True)
            z = z + jnp.exp(mk - m)
            if k + 1 < P_TOPK:
                cur = _remove_first(cur, mk, iota_c, CAND_ROWS)
        tau_ref[h:h + 1, :] = mk
        m_ref[h:h + 1, :] = m
        iz_ref[h:h + 1, :] = 1.0 / z


def peer_select(q, sk1, sk2, *, tm):
    t = q.shape[0]
    big = jax.ShapeDtypeStruct((P_HEADS, N_KEYS, t), F32)
    small = jax.ShapeDtypeStruct((P_HEADS, t), F32)
    big_spec = pl.BlockSpec((P_HEADS, N_KEYS, tm), lambda i: (0, 0, i))
    small_spec = pl.BlockSpec((P_HEADS, tm), lambda i: (0, i))
    key_spec = pl.BlockSpec((P_HEADS, N_KEYS, P_HALF), lambda i: (0, 0, 0))
    return pl.pallas_call(
        _peer_select_kernel,
        grid=(t // tm,),
        in_specs=[pl.BlockSpec((tm, P_HEADS * P_DKEY), lambda i: (i, 0)), key_spec, key_spec],
        out_specs=[big_spec, big_spec, small_spec, small_spec, small_spec],
        out_shape=[big, big, small, small, small],
        scratch_shapes=[pltpu.VMEM((CAND_ROWS, tm), F32)],
        compiler_params=_cparams(("parallel",)),
        name="peer_select",
    )(q, sk1, sk2)


SQRT_HALF = float(np.sqrt(0.5))


def _peer_main_kernel(hnT_ref, u_ref, vt_ref, s1_ref, s2_ref, tau_ref, m_ref, iz_ref, o_ref,
                      a0_ref, a1_ref, p_ref):
    e = pl.program_id(1)
    te, tm = a0_ref.shape
    rows = te // N_KEYS
    half = tm // 2
    prev_tile = jnp.maximum(e - 1, 0)

    @pl.when(e == 0)
    def _():
        o_ref[...] = jnp.zeros_like(o_ref)
        a1_ref[...] = jnp.zeros_like(a1_ref)

    def step(a_cur, a_prev):
        for hs in (slice(0, half), slice(half, tm)):
            a_cur[:, hs] = jnp.dot(u_ref[...], hnT_ref[:, hs], preferred_element_type=F32)
            for r in range(rows):
                i1 = prev_tile * rows + r
                rs = slice(r * N_KEYS, (r + 1) * N_KEYS)
                for c in range(hs.start // LANES, hs.stop // LANES):
                    cs = slice(c * LANES, (c + 1) * LANES)
                    g = jnp.zeros((N_KEYS, LANES), F32)
                    for h in range(P_HEADS):
                        s = s1_ref[i1, h:h + 1, cs] + s2_ref[h, :, cs]
                        w = jnp.exp(s - m_ref[h:h + 1, cs]) * iz_ref[h:h + 1, cs]
                        g = g + jnp.where(s >= tau_ref[h:h + 1, cs], w, 0.0)
                    a = a_prev[rs, cs]
                    act = 0.5 * a * (1.0 + lax.erf(a * SQRT_HALF))
                    p_ref[rs, cs] = (act * g).astype(BF16)
            o_ref[:, hs] += jnp.dot(vt_ref[...], p_ref[:, hs], preferred_element_type=F32)

    @pl.when(e % 2 == 0)
    def _():
        step(a0_ref, a1_ref)

    @pl.when(e % 2 == 1)
    def _():
        step(a1_ref, a0_ref)


def peer_main(hnT, u, vt, s1, s2, tau, m, iz, *, tm, te):
    d, t = hnT.shape
    n_e = N_EXPERTS // te
    small_spec = pl.BlockSpec((P_HEADS, tm), lambda i, e: (0, i))
    big_spec = pl.BlockSpec((P_HEADS, N_KEYS, tm), lambda i, e: (0, 0, i))
    s1_spec = pl.BlockSpec((N_KEYS, P_HEADS, tm), lambda i, e: (0, 0, i))
    return pl.pallas_call(
        _peer_main_kernel,
        grid=(t // tm, n_e + 1),
        in_specs=[pl.BlockSpec((d, tm), lambda i, e: (0, i)),
                  pl.BlockSpec((te, d), lambda i, e: (jnp.minimum(e, n_e - 1), 0)),
                  pl.BlockSpec((d, te), lambda i, e: (0, jnp.maximum(e - 1, 0))),
                  s1_spec, big_spec, small_spec, small_spec, small_spec],
        out_specs=pl.BlockSpec((d, tm), lambda i, e: (0, i)),
        out_shape=jax.ShapeDtypeStruct((d, t), F32),
        scratch_shapes=[pltpu.VMEM((te, tm), F32), pltpu.VMEM((te, tm), F32), pltpu.VMEM((te, tm), BF16)],
        compiler_params=_cparams(("parallel", "arbitrary")),
        name="peer_main",
    )(hnT, u, vt, jnp.transpose(s1, (1, 0, 2)), s2, tau, m, iz)


def kernel(x_prompt, x_sample, mem_prompt, cache_swa_k, cache_swa_v, state_conv, state_gdn, cache_mem_k, cache_mem_v, ln_mix, w_in, conv_w, a_log, dt_bias, gdn_norm, sinks, w_out, ln_cross, ln_mem, w_mq, w_mk, w_mv, w_mo, ln_ffn, w_pq, sub_keys1, sub_keys2, expert_u, expert_v, ln_final):
    Bp, Sp, D = x_prompt.shape
    Bs, Ts, _ = x_sample.shape
    assert Ts == CHUNK and Sp % CHUNK == 0 and cache_swa_k.shape[2] == WINDOW
    n_p = Bp * Sp
    n_s = Bs * Ts
    n_tok = n_p + n_s
    cps = Sp // CHUNK
    x = jnp.concatenate([x_prompt.reshape(n_p, D), x_sample.reshape(n_s, D)], axis=0)
    mem = mem_prompt.reshape(Bp * N_MEM, D)
    conv_pad = ((0, 0), (CONV_ROWS - (CONV_W - 1), 0), (0, 0))

    outs = [[] for _ in range(10)]
    for l in range(DEPTH):
        wi = w_in[l]
        o_qkv = A_WIDTH + 2 * A_KV_WIDTH
        o_ab = o_qkv + B_CONV_CH
        o_z = o_ab + 2 * B_HEADS
        w_in_l = jnp.concatenate(
            [wi[:, o_qkv:o_ab], wi[:, o_z:], wi[:, :o_qkv], wi[:, o_ab:o_z],
             jnp.zeros((D, Z_WIDTH - Z_USED), F32)], axis=1).astype(BF16)
        w_out_l = w_out[l].astype(BF16)
        w_mq_l = w_mq[l].astype(BF16)
        w_mkv_l = jnp.concatenate([w_mk[l], w_mv[l]], axis=1).astype(BF16)
        w_mo_l = w_mo[l].astype(BF16)
        w_pq_l = w_pq[l].astype(BF16)
        u_l = expert_u[l].astype(BF16)
        vt_l = expert_v[l].astype(BF16).T
        sk1_l = sub_keys1[l].astype(BF16)
        sk2_l = sub_keys2[l].astype(BF16)

        zz = norm_matmul(x, ln_mix[l], w_in_l, tm=512, tn=1024)
        ck = cache_swa_k[l].reshape(Bs * WINDOW, A_KV_WIDTH)
        cv = cache_swa_v[l].reshape(Bs * WINDOW, A_KV_WIDTH)
        a_out = swa_prompt(zz, sinks[l], Bp, cps, n_tok)
        a_out = swa_sample(a_out, zz, sinks[l], ck, cv, n_p // CHUNK)
        conv0 = jnp.concatenate([jnp.zeros((Bp, CONV_ROWS, B_CONV_CH), F32),
                                 jnp.pad(state_conv[l], conv_pad)], axis=0)
        s0 = jnp.concatenate([jnp.zeros((Bp, B_HEADS, B_DK, B_DV), F32), state_gdn[l]], axis=0)
        b_out, convst, s_fin = gdn_mixer(zz, conv0, s0, conv_w[l], a_log[l], dt_bias[l], gdn_norm[l],
                                         n_multi_seq=Bp, chunks_per_seq=cps)
        x = mix_out_residual(x, a_out, b_out, w_out_l, tm=512, tn=1024)

        mkv = norm_matmul(mem, ln_mem[l], w_mkv_l, tm=512, tn=1024)
        x = cross_block(x, ln_cross[l], w_mq_l, mkv, mkv, 0, 1, w_mo_l,
                        row0=0, n_rows=n_p, rows_per_batch=Sp, tm=256)
        cmk = cache_mem_k[l].reshape(Bs * N_MEM, M_WIDTH)
        cmv = cache_mem_v[l].reshape(Bs * N_MEM, M_WIDTH)
        x = cross_block(x, ln_cross[l], w_mq_l, cmk, cmv, 0, 0, w_mo_l,
                        row0=n_p, n_rows=n_s, rows_per_batch=Ts, tm=256)

        qp, hnT = norm_matmul(x, ln_ffn[l], w_pq_l, tm=512, tn=512, emit_hn=True)
        s1, s2, tau, m, iz = peer_select(qp, sk1_l, sk2_l, tm=256)
        yT = peer_main(hnT, u_l, vt_l, s1, s2, tau, m, iz, tm=512, te=512)
        x = x + yT.T

        kv_shape = (A_KV_HEADS, A_HEAD_DIM)
        kv_p = lax.slice(zz, (0, Z_AK), (n_p, Z_AB)).reshape(Bp, Sp, 2 * A_KV_WIDTH)[:, Sp - WINDOW:]
        kv_s = lax.slice(zz, (n_p, Z_AK), (n_tok, Z_AB)).reshape(Bs, Ts, 2 * A_KV_WIDTH)
        kb_p = kv_p[..., :A_KV_WIDTH].reshape((Bp, WINDOW) + kv_shape)
        vb_p = kv_p[..., A_KV_WIDTH:].reshape((Bp, WINDOW) + kv_shape)
        kb_s = jnp.concatenate([cache_swa_k[l][:, Ts:], kv_s[..., :A_KV_WIDTH].reshape((Bs, Ts) + kv_shape)], axis=1)
        vb_s = jnp.concatenate([cache_swa_v[l][:, Ts:], kv_s[..., A_KV_WIDTH:].reshape((Bs, Ts) + kv_shape)], axis=1)
        cst = convst[:, CONV_ROWS - (CONV_W - 1):]
        mk = mkv[:, :M_WIDTH].reshape(Bp, N_MEM, M_HEADS, M_HEAD_DIM)
        mv = mkv[:, M_WIDTH:].reshape(Bp, N_MEM, M_HEADS, M_HEAD_DIM)
        for lst, val in zip(outs, (kb_p, vb_p, cst[:Bp], s_fin[:Bp], mk, mv, kb_s, vb_s, cst[Bp:], s_fin[Bp:])):
            lst.append(val)

    y = rmsnorm_rows(x, ln_final, tm=512)
    y_prompt = y[:n_p].reshape(Bp, Sp, D)
    y_sample = y[n_p:].reshape(Bs, Ts, D)
    return (y_prompt, y_sample) + tuple(jnp.stack(o) for o in outs)
```

```python
import functools

import jax
import jax.numpy as jnp
import numpy as np
from jax import lax
from jax.experimental import pallas as pl
from jax.experimental.pallas import tpu as pltpu

F32 = jnp.float32
BF16 = jnp.bfloat16

D_MODEL = 4096
DEPTH = 4
CHUNK = 64
EPS = 1e-6
A_HEADS = 32
A_KV_HEADS = 4
A_GROUP = A_HEADS // A_KV_HEADS
A_HEAD_DIM = 64
WINDOW = 128
WIN_CHUNKS = WINDOW // CHUNK
A_WIDTH = A_HEADS * A_HEAD_DIM
A_KV_WIDTH = A_KV_HEADS * A_HEAD_DIM
B_HEADS = 16
B_DK = 128
B_DV = 128
CONV_W = 4
B_QK_WIDTH = B_HEADS * B_DK
B_WIDTH = B_HEADS * B_DV
B_CONV_CH = 2 * B_QK_WIDTH + B_WIDTH
N_MEM = 256
M_HEADS = 4
M_HEAD_DIM = 128
M_WIDTH = M_HEADS * M_HEAD_DIM
P_HEADS = 8
N_KEYS = 128
N_EXPERTS = N_KEYS * N_KEYS
P_DKEY = 256
P_HALF = P_DKEY // 2
P_TOPK = 16

LANES = 128
SUBLANES = 8

Z_QKV = 0
Z_Z = Z_QKV + B_CONV_CH
Z_AQ = Z_Z + B_WIDTH
Z_AK = Z_AQ + A_WIDTH
Z_AV = Z_AK + A_KV_WIDTH
Z_AB = Z_AV + A_KV_WIDTH
Z_USED = Z_AB + 2 * B_HEADS
Z_WIDTH = 11264

V7X_VMEM_BYTES = 64 * 1024 * 1024
VMEM_LIMIT = V7X_VMEM_BYTES - 8 * 1024 * 1024

NT_DIMS = (((1,), (1,)), ((), ()))
TN_DIMS = (((0,), (0,)), ((), ()))


def _cparams(sem):
    return pltpu.CompilerParams(dimension_semantics=sem, vmem_limit_bytes=VMEM_LIMIT)


def _rms_rows(x, g):
    ms = jnp.mean(x * x, axis=-1, keepdims=True)
    return x * lax.rsqrt(ms + EPS) * g


def _norm_matmul_kernel(x_ref, g_ref, w_ref, o_ref, *rest, emit_hn):
    hn_ref = rest[-1]
    j = pl.program_id(1)

    @pl.when(j == 0)
    def _():
        hn = _rms_rows(x_ref[...], g_ref[...]).astype(BF16)
        hn_ref[...] = hn
        if emit_hn:
            rest[0][...] = hn.T

    o_ref[...] = jnp.dot(hn_ref[...], w_ref[...], preferred_element_type=F32)


def norm_matmul(x, g, w, *, tm, tn, emit_hn=False):
    t, d = x.shape
    n = w.shape[1]
    out_shape = [jax.ShapeDtypeStruct((t, n), F32)]
    out_specs = [pl.BlockSpec((tm, tn), lambda i, j: (i, j))]
    if emit_hn:
        out_shape.append(jax.ShapeDtypeStruct((d, t), BF16))
        out_specs.append(pl.BlockSpec((d, tm), lambda i, j: (0, i)))
    res = pl.pallas_call(
        functools.partial(_norm_matmul_kernel, emit_hn=emit_hn),
        grid=(t // tm, n // tn),
        in_specs=[pl.BlockSpec((tm, d), lambda i, j: (i, 0)),
                  pl.BlockSpec((1, d), lambda i, j: (0, 0)),
                  pl.BlockSpec((d, tn), lambda i, j: (0, j))],
        out_specs=out_specs,
        out_shape=out_shape,
        scratch_shapes=[pltpu.VMEM((tm, d), BF16)],
        compiler_params=_cparams(("parallel", "arbitrary")),
        name="norm_matmul",
    )(x, g.reshape(1, d), w)
    return res if emit_hn else res[0]


def _mix_out_kernel(x_ref, a_ref, b_ref, wa_ref, wb_ref, o_ref):
    acc = jnp.dot(a_ref[...], wa_ref[...], preferred_element_type=F32)
    acc = acc + jnp.dot(b_ref[...], wb_ref[...], preferred_element_type=F32)
    o_ref[...] = x_ref[...] + acc


def mix_out_residual(x, a, b, w, *, tm, tn):
    t, n = x.shape
    ka, kb = a.shape[1], b.shape[1]
    assert ka == kb and w.shape[0] == ka + kb
    return pl.pallas_call(
        _mix_out_kernel,
        grid=(t // tm, n // tn),
        in_specs=[pl.BlockSpec((tm, tn), lambda i, j: (i, j)),
                  pl.BlockSpec((tm, ka), lambda i, j: (i, 0)),
                  pl.BlockSpec((tm, kb), lambda i, j: (i, 0)),
                  pl.BlockSpec((ka, tn), lambda i, j: (0, j)),
                  pl.BlockSpec((kb, tn), lambda i, j: (1, j))],
        out_specs=pl.BlockSpec((tm, tn), lambda i, j: (i, j)),
        out_shape=jax.ShapeDtypeStruct((t, n), F32),
        compiler_params=_cparams(("parallel", "arbitrary")),
        name="mix_out_residual",
    )(x, a, b, w, w)


def _rmsnorm_kernel(x_ref, g_ref, o_ref):
    o_ref[...] = _rms_rows(x_ref[...], g_ref[...])


def rmsnorm_rows(x, g, *, tm):
    t, d = x.shape
    return pl.pallas_call(
        _rmsnorm_kernel,
        grid=(t // tm,),
        in_specs=[pl.BlockSpec((tm, d), lambda i: (i, 0)),
                  pl.BlockSpec((1, d), lambda i: (0, 0))],
        out_specs=pl.BlockSpec((tm, d), lambda i: (i, 0)),
        out_shape=jax.ShapeDtypeStruct((t, d), F32),
        compiler_params=_cparams(("parallel",)),
        name="final_rmsnorm",
    )(x, g.reshape(1, d))


ALIBI_SLOPES = [float(2.0 ** (-8.0 * (n + 1) / A_HEADS)) for n in range(A_HEADS)]
BAND = (WIN_CHUNKS + 1) * CHUNK


def _swa_kernel(sink_ref, q_ref, k0_ref, k1_ref, k2_ref, v0_ref, v1_ref, v2_ref, o_ref, *,
                chunks_per_seq):
    qi = lax.broadcasted_iota(jnp.int32, (CHUNK, BAND), 0)
    kj = lax.broadcasted_iota(jnp.int32, (CHUNK, BAND), 1)
    dist = jnp.abs(qi + WINDOW - kj).astype(F32)
    if chunks_per_seq is None:
        valid = None
    else:
        c = pl.program_id(0) % chunks_per_seq
        valid = kj >= (WIN_CHUNKS - jnp.minimum(c, WIN_CHUNKS)) * CHUNK
    scale = A_HEAD_DIM ** -0.5
    scores, vals = [], []
    for h in range(A_KV_HEADS):
        hs = slice(h * A_HEAD_DIM, (h + 1) * A_HEAD_DIM)
        qh = jnp.concatenate(
            [q_ref[:, (h * A_GROUP + g) * A_HEAD_DIM:(h * A_GROUP + g + 1) * A_HEAD_DIM]
             for g in range(A_GROUP)], axis=0).astype(BF16)
        kh = jnp.concatenate([k0_ref[:, hs], k1_ref[:, hs], k2_ref[:, hs]], axis=0).astype(BF16)
        vals.append(jnp.concatenate([v0_ref[:, hs], v1_ref[:, hs], v2_ref[:, hs]], axis=0).astype(BF16))
        scores.append(lax.dot_general(qh, kh, NT_DIMS, preferred_element_type=F32) * scale)
    probs = []
    for h in range(A_KV_HEADS):
        ps = []
        for g in range(A_GROUP):
            n = h * A_GROUP + g
            sg = scores[h][g * CHUNK:(g + 1) * CHUNK] - ALIBI_SLOPES[n] * dist
            if valid is not None:
                sg = jnp.where(valid, sg, -jnp.inf)
            sk = sink_ref[n]
            m = jnp.maximum(jnp.max(sg, axis=-1, keepdims=True), sk)
            p = jnp.exp(sg - m)
            p = p / (jnp.sum(p, axis=-1, keepdims=True) + jnp.exp(sk - m))
            ps.append(p.astype(BF16))
        probs.append(jnp.concatenate(ps, axis=0))
    for h in range(A_KV_HEADS):
        oh = jnp.dot(probs[h], vals[h], preferred_element_type=F32)
        for g in range(A_GROUP):
            n = h * A_GROUP + g
            o_ref[:, n * A_HEAD_DIM:(n + 1) * A_HEAD_DIM] = oh[g * CHUNK:(g + 1) * CHUNK].astype(BF16)


def swa_prompt(zz, sinks, n_seq, chunks_per_seq, total_rows):
    qb, kb, vb = Z_AQ // A_WIDTH, Z_AK // A_KV_WIDTH, Z_AV // A_KV_WIDTH

    def hist(j, col):
        def index(i):
            c = i % chunks_per_seq
            return (i - jnp.minimum(c, WIN_CHUNKS - j), col)
        return pl.BlockSpec((CHUNK, A_KV_WIDTH), index)

    return pl.pallas_call(
        functools.partial(_swa_kernel, chunks_per_seq=chunks_per_seq),
        grid=(n_seq * chunks_per_seq,),
        in_specs=[pl.BlockSpec(memory_space=pltpu.SMEM),
                  pl.BlockSpec((CHUNK, A_WIDTH), lambda i: (i, qb)),
                  hist(0, kb), hist(1, kb), hist(2, kb),
                  hist(0, vb), hist(1, vb), hist(2, vb)],
        out_specs=pl.BlockSpec((CHUNK, A_WIDTH), lambda i: (i, 0)),
        out_shape=jax.ShapeDtypeStruct((total_rows, A_WIDTH), BF16),
        compiler_params=_cparams(("parallel",)),
        name="swa_prompt",
    )(sinks, zz, zz, zz, zz, zz, zz, zz)


def swa_sample(a_out, zz, sinks, ck, cv, row_chunk0):
    n_seq = ck.shape[0] // WINDOW
    qb, kb, vb = Z_AQ // A_WIDTH, Z_AK // A_KV_WIDTH, Z_AV // A_KV_WIDTH
    cache0 = pl.BlockSpec((CHUNK, A_KV_WIDTH), lambda i: (WIN_CHUNKS * i, 0))
    cache1 = pl.BlockSpec((CHUNK, A_KV_WIDTH), lambda i: (WIN_CHUNKS * i + 1, 0))

    def body(sink_ref, q_ref, k0, k1, k2, v0, v1, v2, prev_ref, o_ref):
        del prev_ref
        _swa_kernel(sink_ref, q_ref, k0, k1, k2, v0, v1, v2, o_ref, chunks_per_seq=None)

    return pl.pallas_call(
        body,
        grid=(n_seq,),
        in_specs=[pl.BlockSpec(memory_space=pltpu.SMEM),
                  pl.BlockSpec((CHUNK, A_WIDTH), lambda i: (row_chunk0 + i, qb)),
                  cache0, cache1, pl.BlockSpec((CHUNK, A_KV_WIDTH), lambda i: (row_chunk0 + i, kb)),
                  cache0, cache1, pl.BlockSpec((CHUNK, A_KV_WIDTH), lambda i: (row_chunk0 + i, vb)),
                  pl.BlockSpec(memory_space=pl.ANY)],
        out_specs=pl.BlockSpec((CHUNK, A_WIDTH), lambda i: (row_chunk0 + i, 0)),
        out_shape=jax.ShapeDtypeStruct(a_out.shape, BF16),
        input_output_aliases={8: 0},
        compiler_params=_cparams(("parallel",)),
        name="swa_sample",
    )(sinks, zz, ck, ck, zz, cv, cv, zz, a_out)


CONV_ROWS = SUBLANES


def _silu(x):
    return x * jax.nn.sigmoid(x)


def _gdn_kernel(qkv_ref, z_ref, ab_ref, conv0_ref, s0_ref, cw_ref, alog_ref, dtb_ref, ng_ref,
                o_ref, convst_ref, s_ref, xext_ref, *, n_multi, chunks_per_seq):
    i = pl.program_id(0)
    in_multi = i < n_multi
    c = i % chunks_per_seq
    first = jnp.logical_or(jnp.logical_not(in_multi), c == 0)
    last = jnp.logical_or(jnp.logical_not(in_multi), c == chunks_per_seq - 1)

    @pl.when(first)
    def _():
        xext_ref[0:CONV_ROWS, :] = conv0_ref[0]
        s_ref[...] = s0_ref[...]

    xext_ref[CONV_ROWS:CONV_ROWS + CHUNK, :] = qkv_ref[...]

    ri = lax.broadcasted_iota(jnp.int32, (CHUNK, CHUNK), 0)
    ci = lax.broadcasted_iota(jnp.int32, (CHUNK, CHUNK), 1)
    tri_incl = ri >= ci
    tri_strict = ri > ci

    ab = ab_ref[...]
    x = ab + dtb_ref[...]
    softplus = jnp.maximum(x, 0.0) + jnp.log1p(jnp.exp(-jnp.abs(x)))
    gate = -jnp.exp(alog_ref[...]) * softplus
    beta_all = jax.nn.sigmoid(ab)
    gcum = jnp.dot(tri_incl.astype(F32), gate, preferred_element_type=F32,
                   precision=lax.Precision.HIGHEST)
    gcum_t = gcum.T
    egc = jnp.exp(gcum)
    glast = gcum[CHUNK - 1:CHUNK, :]
    etail = jnp.exp(glast - gcum)
    eglast = jnp.exp(glast)

    def conv(col0):
        cs = slice(col0, col0 + LANES)
        y = xext_ref[CONV_ROWS:CONV_ROWS + CHUNK, cs] * cw_ref[CONV_W - 1:CONV_W, cs]
        for j in range(1, CONV_W):
            y = y + xext_ref[CONV_ROWS - j:CONV_ROWS - j + CHUNK, cs] * cw_ref[CONV_W - 1 - j:CONV_W - j, cs]
        return _silu(y)

    def l2n(a):
        return a * lax.rsqrt(jnp.sum(a * a, axis=-1, keepdims=True) + EPS)

    heads = range(B_HEADS)
    qs, ks, attns, xss, ms = [], [], [], [], []
    for h in heads:
        q = l2n(conv(h * B_DK)) * (B_DK ** -0.5)
        k = l2n(conv(B_QK_WIDTH + h * B_DK))
        v = conv(2 * B_QK_WIDTH + h * B_DV)
        beta = beta_all[:, B_HEADS + h:B_HEADS + h + 1]
        gc_col = gcum[:, h:h + 1]
        gc_row = gcum_t[h:h + 1, :]
        decay = jnp.exp(jnp.where(tri_incl, gc_col - gc_row, -jnp.inf))
        kbeta = k * beta
        kk_qk = lax.dot_general(jnp.concatenate([kbeta, q], axis=0).astype(BF16), k.astype(BF16),
                                NT_DIMS, preferred_element_type=F32)
        ms.append(-jnp.where(tri_strict, kk_qk[:CHUNK] * decay, 0.0))
        attns.append((kk_qk[CHUNK:] * decay).astype(BF16))
        xss.append(jnp.concatenate([v * beta, kbeta * egc[:, h:h + 1]], axis=1))
        qs.append((q * egc[:, h:h + 1]).astype(BF16))
        ks.append((k * etail[:, h:h + 1]).astype(BF16))
    for _ in range(6):
        for h in heads:
            mb = ms[h].astype(BF16)
            y = jnp.dot(mb, jnp.concatenate([xss[h].astype(BF16), mb], axis=1), preferred_element_type=F32)
            xss[h] = xss[h] + y[:, :2 * B_DV]
            ms[h] = y[:, 2 * B_DV:]
    v_news, os_ = [], []
    for h in heads:
        u, w = xss[h][:, :B_DV], xss[h][:, B_DV:]
        ws_qs = jnp.dot(jnp.concatenate([w.astype(BF16), qs[h]], axis=0),
                        s_ref[0, h].astype(BF16), preferred_element_type=F32)
        v_news.append((u - ws_qs[:CHUNK]).astype(BF16))
        os_.append(ws_qs[CHUNK:])
    for h in heads:
        o = os_[h] + jnp.dot(attns[h], v_news[h], preferred_element_type=F32)
        s_ref[0, h] = s_ref[0, h] * eglast[:, h:h + 1] + lax.dot_general(
            ks[h], v_news[h], TN_DIMS, preferred_element_type=F32)
        zs = slice(h * B_DV, (h + 1) * B_DV)
        o_ref[:, zs] = (_rms_rows(o, ng_ref[...]) * _silu(z_ref[:, zs])).astype(BF16)

    xext_ref[0:CONV_ROWS, :] = xext_ref[CHUNK:CHUNK + CONV_ROWS, :]

    @pl.when(last)
    def _():
        convst_ref[0] = xext_ref[0:CONV_ROWS, :]


def gdn_mixer(zz, conv0, s0, conv_w, a_log, dt_bias, norm_g, *, n_multi_seq, chunks_per_seq):
    t = zz.shape[0]
    n_seq = s0.shape[0]
    n_multi = n_multi_seq * chunks_per_seq
    n_steps = t // CHUNK
    assert n_steps == n_multi + (n_seq - n_multi_seq)

    def seq(i):
        return jnp.where(i < n_multi, i // chunks_per_seq, i - n_multi + n_multi_seq)

    pad = lambda a: jnp.pad(a.astype(F32), (0, LANES - a.shape[0])).reshape(1, LANES)
    kern = functools.partial(_gdn_kernel, n_multi=n_multi, chunks_per_seq=chunks_per_seq)
    return pl.pallas_call(
        kern,
        grid=(n_steps,),
        in_specs=[pl.BlockSpec((CHUNK, B_CONV_CH), lambda i: (i, Z_QKV // B_CONV_CH)),
                  pl.BlockSpec((CHUNK, B_WIDTH), lambda i: (i, Z_Z // B_WIDTH)),
                  pl.BlockSpec((CHUNK, LANES), lambda i: (i, Z_AB // LANES)),
                  pl.BlockSpec((1, CONV_ROWS, B_CONV_CH), lambda i: (seq(i), 0, 0)),
                  pl.BlockSpec((1, B_HEADS, B_DK, B_DV), lambda i: (seq(i), 0, 0, 0)),
                  pl.BlockSpec((CONV_W, B_CONV_CH), lambda i: (0, 0)),
                  pl.BlockSpec((1, LANES), lambda i: (0, 0)),
                  pl.BlockSpec((1, LANES), lambda i: (0, 0)),
                  pl.BlockSpec((1, B_DV), lambda i: (0, 0))],
        out_specs=[pl.BlockSpec((CHUNK, B_WIDTH), lambda i: (i, 0)),
                   pl.BlockSpec((1, CONV_ROWS, B_CONV_CH), lambda i: (seq(i), 0, 0)),
                   pl.BlockSpec((1, B_HEADS, B_DK, B_DV), lambda i: (seq(i), 0, 0, 0))],
        out_shape=[jax.ShapeDtypeStruct((t, B_WIDTH), BF16),
                   jax.ShapeDtypeStruct((n_seq, CONV_ROWS, B_CONV_CH), F32),
                   jax.ShapeDtypeStruct((n_seq, B_HEADS, B_DK, B_DV), F32)],
        scratch_shapes=[pltpu.VMEM((CONV_ROWS + CHUNK, B_CONV_CH), F32)],
        compiler_params=_cparams(("arbitrary",)),
        name="gdn_mixer",
    )(zz, zz, zz, conv0, s0, conv_w, pad(a_log), pad(dt_bias), norm_g.reshape(1, B_DV))


def _cross_kernel(x_ref, g_ref, wq_ref, mk_ref, mv_ref, wo_ref, o_ref, *, n_batch):
    x = x_ref[...]
    tm = x.shape[0]
    rows = tm // n_batch
    hn = _rms_rows(x, g_ref[...]).astype(BF16)
    q = jnp.dot(hn, wq_ref[...], preferred_element_type=F32)
    scale = M_HEAD_DIM ** -0.5
    pairs = [(b, h) for b in range(n_batch) for h in range(M_HEADS)]
    scores, vals = {}, {}
    for b, h in pairs:
        hs = slice(h * M_HEAD_DIM, (h + 1) * M_HEAD_DIM)
        qh = q[b * rows:(b + 1) * rows, hs].astype(BF16)
        kh = mk_ref[b * N_MEM:(b + 1) * N_MEM, hs].astype(BF16)
        vals[b, h] = mv_ref[b * N_MEM:(b + 1) * N_MEM, hs].astype(BF16)
        scores[b, h] = lax.dot_general(qh, kh, NT_DIMS, preferred_element_type=F32) * scale
    probs = {}
    for b, h in pairs:
        s = scores[b, h]
        p = jnp.exp(s - jnp.max(s, axis=-1, keepdims=True))
        probs[b, h] = (p / jnp.sum(p, axis=-1, keepdims=True)).astype(BF16)
    outs = []
    for b in range(n_batch):
        outs.append(jnp.concatenate(
            [jnp.dot(probs[b, h], vals[b, h], preferred_element_type=F32) for h in range(M_HEADS)], axis=1))
    o = jnp.concatenate(outs, axis=0) if n_batch > 1 else outs[0]
    o_ref[...] = x + jnp.dot(o.astype(BF16), wo_ref[...], preferred_element_type=F32)


def cross_block(x, g, wq, mk, mv, mk_col, mv_col, wo, *, row0, n_rows, rows_per_batch, tm):
    t, d = x.shape
    if rows_per_batch >= tm:
        n_batch = 1
        tiles_per_batch = rows_per_batch // tm
        mem_index = lambda col: (lambda i: (i // tiles_per_batch, col))
    else:
        n_batch = tm // rows_per_batch
        mem_index = lambda col: (lambda i: (i, col))
    r0 = row0 // tm
    return pl.pallas_call(
        functools.partial(_cross_kernel, n_batch=n_batch),
        grid=(n_rows // tm,),
        in_specs=[pl.BlockSpec((tm, d), lambda i: (r0 + i, 0)),
                  pl.BlockSpec((1, d), lambda i: (0, 0)),
                  pl.BlockSpec((d, M_WIDTH), lambda i: (0, 0)),
                  pl.BlockSpec((n_batch * N_MEM, M_WIDTH), mem_index(mk_col)),
                  pl.BlockSpec((n_batch * N_MEM, M_WIDTH), mem_index(mv_col)),
                  pl.BlockSpec((M_WIDTH, d), lambda i: (0, 0))],
        out_specs=pl.BlockSpec((tm, d), lambda i: (r0 + i, 0)),
        out_shape=jax.ShapeDtypeStruct((t, d), F32),
        input_output_aliases={0: 0},
        compiler_params=_cparams(("parallel",)),
        name="cross_block",
    )(x, g.reshape(1, d), wq, mk, mv, wo)


def _batcher_network(n):
    pairs = []
    t = n.bit_length() - 1
    p = 1 << (t - 1)
    while p > 0:
        q, r, d = 1 << (t - 1), 0, p
        while d > 0:
            for i in range(n - d):
                if i & p == r:
                    pairs.append((i, i + d))
            d, q, r = q - p, q >> 1, p
        p >>= 1
    return pairs


SORT16 = _batcher_network(P_TOPK)


def _cmpx(vals, i, j):
    vals[i], vals[j] = jnp.maximum(vals[i], vals[j]), jnp.minimum(vals[i], vals[j])


def _sort_bitonic(vals):
    stride = P_TOPK // 2
    while stride:
        for i in range(P_TOPK):
            if (i // stride) % 2 == 0:
                _cmpx(vals, i, i + stride)
        stride //= 2
    return vals


def _merge_sublanes(vals):
    for shift in (4, 2, 1):
        other = [pltpu.roll(v, shift, axis=0) for v in vals]
        vals = _sort_bitonic([jnp.maximum(vals[i], other[P_TOPK - 1 - i]) for i in range(P_TOPK)])
    return vals


def _peer_select_kernel(q_ref, sk1_ref, sk2_ref, s1_ref, s2_ref, tau_ref, m_ref, iz_ref):
    tm = q_ref.shape[0]
    sub = lax.broadcasted_iota(jnp.int32, (SUBLANES, tm), 0)
    neg = jnp.full((SUBLANES, tm), -jnp.inf, F32)

    def top16(s):
        vals = [s[SUBLANES * j:SUBLANES * (j + 1)] for j in range(N_KEYS // SUBLANES)]
        for i, j in SORT16:
            _cmpx(vals, i, j)
        return _merge_sublanes(vals)

    def spread(vals):
        out = vals[0]
        for b in range(1, SUBLANES):
            out = jnp.where(sub == b, vals[b], out)
        return out

    for h in range(P_HEADS):
        q1 = q_ref[:, h * P_DKEY:h * P_DKEY + P_HALF].astype(BF16)
        q2 = q_ref[:, h * P_DKEY + P_HALF:(h + 1) * P_DKEY].astype(BF16)
        s1 = lax.dot_general(sk1_ref[h], q1, NT_DIMS, preferred_element_type=F32)
        s2 = lax.dot_general(sk2_ref[h], q2, NT_DIMS, preferred_element_type=F32)
        s1_ref[h] = s1
        s2_ref[h] = s2
        t1 = top16(s1)
        t2 = top16(s2)
        t2_lo, t2_hi = spread(t2[:SUBLANES]), spread(t2[SUBLANES:])
        cand = [jnp.where((a + 1) * (sub + 1) <= P_TOPK, t1[a] + t2_lo, neg) for a in range(P_TOPK)]
        cand[P_TOPK - 1] = jnp.maximum(cand[P_TOPK - 1], t1[0] + t2_hi)
        best = _merge_sublanes(_sort_bitonic(cand))
        m = best[0][0:1, :]
        z = jnp.zeros((1, tm), F32)
        for k in range(P_TOPK):
            z = z + jnp.exp(best[k][0:1, :] - m)
        tau_ref[h:h + 1, :] = best[P_TOPK - 1][0:1, :]
        m_ref[h:h + 1, :] = m
        iz_ref[h:h + 1, :] = 1.0 / z


def peer_select(q, sk1, sk2, *, tm):
    t = q.shape[0]
    big = jax.ShapeDtypeStruct((P_HEADS, N_KEYS, t), F32)
    small = jax.ShapeDtypeStruct((P_HEADS, t), F32)
    big_spec = pl.BlockSpec((P_HEADS, N_KEYS, tm), lambda i: (0, 0, i))
    small_spec = pl.BlockSpec((P_HEADS, tm), lambda i: (0, i))
    key_spec = pl.BlockSpec((P_HEADS, N_KEYS, P_HALF), lambda i: (0, 0, 0))
    return pl.pallas_call(
        _peer_select_kernel,
        grid=(t // tm,),
        in_specs=[pl.BlockSpec((tm, P_HEADS * P_DKEY), lambda i: (i, 0)), key_spec, key_spec],
        out_specs=[big_spec, big_spec, small_spec, small_spec, small_spec],
        out_shape=[big, big, small, small, small],
        compiler_params=_cparams(("parallel",)),
        name="peer_select",
    )(q, sk1, sk2)


SQRT_HALF = float(np.sqrt(0.5))


def _peer_main_kernel(hnT_ref, u_ref, vt_ref, s1_ref, s2_ref, tau_ref, m_ref, iz_ref, o_ref,
                      a0_ref, a1_ref, p_ref):
    e = pl.program_id(1)
    te, tm = a0_ref.shape
    rows = te // N_KEYS
    half = tm // 2
    prev_tile = jnp.maximum(e - 1, 0)

    @pl.when(e == 0)
    def _():
        o_ref[...] = jnp.zeros_like(o_ref)
        a1_ref[...] = jnp.zeros_like(a1_ref)

    def step(a_cur, a_prev, first_matmul):
        for hs in (slice(0, half), slice(half, tm)):
            if first_matmul:
                a_cur[:, hs] = jnp.dot(u_ref[...], hnT_ref[:, hs], preferred_element_type=F32)
            for r in range(rows):
                i1 = prev_tile * rows + r
                rs = slice(r * N_KEYS, (r + 1) * N_KEYS)
                for c in range(hs.start // LANES, hs.stop // LANES):
                    cs = slice(c * LANES, (c + 1) * LANES)
                    g = jnp.zeros((N_KEYS, LANES), F32)
                    for h in range(P_HEADS):
                        s = s1_ref[i1, h:h + 1, cs] + s2_ref[h, :, cs]
                        w = jnp.exp(s - m_ref[h:h + 1, cs]) * iz_ref[h:h + 1, cs]
                        g = g + jnp.where(s >= tau_ref[h:h + 1, cs], w, 0.0)
                    a = a_prev[rs, cs]
                    act = 0.5 * a * (1.0 + lax.erf(a * SQRT_HALF))
                    p_ref[rs, cs] = (act * g).astype(BF16)
            o_ref[:, hs] += jnp.dot(vt_ref[...], p_ref[:, hs], preferred_element_type=F32)

    last = pl.num_programs(1) - 1

    @pl.when(jnp.logical_and(e % 2 == 0, e < last))
    def _():
        step(a0_ref, a1_ref, True)

    @pl.when(jnp.logical_and(e % 2 == 1, e < last))
    def _():
        step(a1_ref, a0_ref, True)

    @pl.when(jnp.logical_and(e % 2 == 0, e == last))
    def _():
        step(a0_ref, a1_ref, False)

    @pl.when(jnp.logical_and(e % 2 == 1, e == last))
    def _():
        step(a1_ref, a0_ref, False)


def peer_main(hnT, u, vt, s1, s2, tau, m, iz, *, tm, te):
    d, t = hnT.shape
    n_e = N_EXPERTS // te
    small_spec = pl.BlockSpec((P_HEADS, tm), lambda i, e: (0, i))
    big_spec = pl.BlockSpec((P_HEADS, N_KEYS, tm), lambda i, e: (0, 0, i), pipeline_mode=pl.Buffered(1))
    s1_spec = pl.BlockSpec((N_KEYS, P_HEADS, tm), lambda i, e: (0, 0, i), pipeline_mode=pl.Buffered(1))
    return pl.pallas_call(
        _peer_main_kernel,
        grid=(t // tm, n_e + 1),
        in_specs=[pl.BlockSpec((d, tm), lambda i, e: (0, i)),
                  pl.BlockSpec((te, d), lambda i, e: (jnp.minimum(e, n_e - 1), 0)),
                  pl.BlockSpec((d, te), lambda i, e: (0, jnp.maximum(e - 1, 0))),
                  s1_spec, big_spec, small_spec, small_spec, small_spec],
        out_specs=pl.BlockSpec((d, tm), lambda i, e: (0, i)),
        out_shape=jax.ShapeDtypeStruct((d, t), F32),
        scratch_shapes=[pltpu.VMEM((te, tm), F32), pltpu.VMEM((te, tm), F32), pltpu.VMEM((te, tm), BF16)],
        compiler_params=_cparams(("parallel", "arbitrary")),
        name="peer_main",
    )(hnT, u, vt, jnp.transpose(s1, (1, 0, 2)), s2, tau, m, iz)


def _residual_t_kernel(x_ref, yt_ref, o_ref):
    o_ref[...] = x_ref[...] + yt_ref[...].T


def residual_add_transposed(x, yT, *, tm):
    t, d = x.shape
    return pl.pallas_call(
        _residual_t_kernel,
        grid=(t // tm,),
        in_specs=[pl.BlockSpec((tm, d), lambda i: (i, 0)),
                  pl.BlockSpec((d, tm), lambda i: (0, i))],
        out_specs=pl.BlockSpec((tm, d), lambda i: (i, 0)),
        out_shape=jax.ShapeDtypeStruct((t, d), F32),
        input_output_aliases={0: 0},
        compiler_params=_cparams(("parallel",)),
        name="residual_add_transposed",
    )(x, yT)


def kernel(x_prompt, x_sample, mem_prompt, cache_swa_k, cache_swa_v, state_conv, state_gdn, cache_mem_k, cache_mem_v, ln_mix, w_in, conv_w, a_log, dt_bias, gdn_norm, sinks, w_out, ln_cross, ln_mem, w_mq, w_mk, w_mv, w_mo, ln_ffn, w_pq, sub_keys1, sub_keys2, expert_u, expert_v, ln_final):
    Bp, Sp, D = x_prompt.shape
    Bs, Ts, _ = x_sample.shape
    assert Ts == CHUNK and Sp % CHUNK == 0 and cache_swa_k.shape[2] == WINDOW
    n_p = Bp * Sp
    n_s = Bs * Ts
    n_tok = n_p + n_s
    cps = Sp // CHUNK
    x = jnp.concatenate([x_prompt.reshape(n_p, D), x_sample.reshape(n_s, D)], axis=0)
    mem = mem_prompt.reshape(Bp * N_MEM, D)
    conv_pad = ((0, 0), (CONV_ROWS - (CONV_W - 1), 0), (0, 0))

    outs = [[] for _ in range(10)]
    for l in range(DEPTH):
        wi = w_in[l]
        o_qkv = A_WIDTH + 2 * A_KV_WIDTH
        o_ab = o_qkv + B_CONV_CH
        o_z = o_ab + 2 * B_HEADS
        w_in_l = jnp.concatenate(
            [wi[:, o_qkv:o_ab], wi[:, o_z:], wi[:, :o_qkv], wi[:, o_ab:o_z],
             jnp.zeros((D, Z_WIDTH - Z_USED), F32)], axis=1).astype(BF16)
        w_out_l = w_out[l].astype(BF16)
        w_mq_l = w_mq[l].astype(BF16)
        w_mkv_l = jnp.concatenate([w_mk[l], w_mv[l]], axis=1).astype(BF16)
        w_mo_l = w_mo[l].astype(BF16)
        w_pq_l = w_pq[l].astype(BF16)
        u_l = expert_u[l].astype(BF16)
        vt_l = expert_v[l].astype(BF16).T
        sk1_l = sub_keys1[l].astype(BF16)
        sk2_l = sub_keys2[l].astype(BF16)

        zz = norm_matmul(x, ln_mix[l], w_in_l, tm=512, tn=1024)
        ck = cache_swa_k[l].reshape(Bs * WINDOW, A_KV_WIDTH)
        cv = cache_swa_v[l].reshape(Bs * WINDOW, A_KV_WIDTH)
        a_out = swa_prompt(zz, sinks[l], Bp, cps, n_tok)
        a_out = swa_sample(a_out, zz, sinks[l], ck, cv, n_p // CHUNK)
        conv0 = jnp.concatenate([jnp.zeros((Bp, CONV_ROWS, B_CONV_CH), F32),
                                 jnp.pad(state_conv[l], conv_pad)], axis=0)
        s0 = jnp.concatenate([jnp.zeros((Bp, B_HEADS, B_DK, B_DV), F32), state_gdn[l]], axis=0)
        b_out, convst, s_fin = gdn_mixer(zz, conv0, s0, conv_w[l], a_log[l], dt_bias[l], gdn_norm[l],
                                         n_multi_seq=Bp, chunks_per_seq=cps)
        x = mix_out_residual(x, a_out, b_out, w_out_l, tm=512, tn=1024)

        mkv = norm_matmul(mem, ln_mem[l], w_mkv_l, tm=512, tn=1024)
        x = cross_block(x, ln_cross[l], w_mq_l, mkv, mkv, 0, 1, w_mo_l,
                        row0=0, n_rows=n_p, rows_per_batch=Sp, tm=256)
        cmk = cache_mem_k[l].reshape(Bs * N_MEM, M_WIDTH)
        cmv = cache_mem_v[l].reshape(Bs * N_MEM, M_WIDTH)
        x = cross_block(x, ln_cross[l], w_mq_l, cmk, cmv, 0, 0, w_mo_l,
                        row0=n_p, n_rows=n_s, rows_per_batch=Ts, tm=256)

        qp, hnT = norm_matmul(x, ln_ffn[l], w_pq_l, tm=512, tn=512, emit_hn=True)
        s1, s2, tau, m, iz = peer_select(qp, sk1_l, sk2_l, tm=256)
        yT = peer_main(hnT, u_l, vt_l, s1, s2, tau, m, iz, tm=512, te=512)
        x = residual_add_transposed(x, yT, tm=256)

        kv_shape = (A_KV_HEADS, A_HEAD_DIM)
        kv_p = lax.slice(zz, (0, Z_AK), (n_p, Z_AB)).reshape(Bp, Sp, 2 * A_KV_WIDTH)[:, Sp - WINDOW:]
        kv_s = lax.slice(zz, (n_p, Z_AK), (n_tok, Z_AB)).reshape(Bs, Ts, 2 * A_KV_WIDTH)
        kb_p = kv_p[..., :A_KV_WIDTH].reshape((Bp, WINDOW) + kv_shape)
        vb_p = kv_p[..., A_KV_WIDTH:].reshape((Bp, WINDOW) + kv_shape)
        kb_s = jnp.concatenate([cache_swa_k[l][:, Ts:], kv_s[..., :A_KV_WIDTH].reshape((Bs, Ts) + kv_shape)], axis=1)
        vb_s = jnp.concatenate([cache_swa_v[l][:, Ts:], kv_s[..., A_KV_WIDTH:].reshape((Bs, Ts) + kv_shape)], axis=1)
        cst = convst[:, CONV_ROWS - (CONV_W - 1):]
        mk = mkv[:, :M_WIDTH].reshape(Bp, N_MEM, M_HEADS, M_HEAD_DIM)
        mv = mkv[:, M_WIDTH:].reshape(Bp, N_MEM, M_HEADS, M_HEAD_DIM)
        for lst, val in zip(outs, (kb_p, vb_p, cst[:Bp], s_fin[:Bp], mk, mv, kb_s, vb_s, cst[Bp:], s_fin[Bp:])):
            lst.append(val)

    y = rmsnorm_rows(x, ln_final, tm=512)
    y_prompt = y[:n_p].reshape(Bp, Sp, D)
    y_sample = y[n_p:].reshape(Bs, Ts, D)
    return (y_prompt, y_sample) + tuple(jnp.stack(o) for o in outs)
```

```python
import functools

import jax
import jax.numpy as jnp
import numpy as np
from jax import lax
from jax.experimental import pallas as pl
from jax.experimental.pallas import tpu as pltpu

F32 = jnp.float32
BF16 = jnp.bfloat16

D_MODEL = 4096
DEPTH = 4
CHUNK = 64
EPS = 1e-6
A_HEADS = 32
A_KV_HEADS = 4
A_GROUP = A_HEADS // A_KV_HEADS
A_HEAD_DIM = 64
WINDOW = 128
WIN_CHUNKS = WINDOW // CHUNK
A_WIDTH = A_HEADS * A_HEAD_DIM
A_KV_WIDTH = A_KV_HEADS * A_HEAD_DIM
B_HEADS = 16
B_DK = 128
B_DV = 128
CONV_W = 4
B_QK_WIDTH = B_HEADS * B_DK
B_WIDTH = B_HEADS * B_DV
B_CONV_CH = 2 * B_QK_WIDTH + B_WIDTH
N_MEM = 256
M_HEADS = 4
M_HEAD_DIM = 128
M_WIDTH = M_HEADS * M_HEAD_DIM
P_HEADS = 8
N_KEYS = 128
N_EXPERTS = N_KEYS * N_KEYS
P_DKEY = 256
P_HALF = P_DKEY // 2
P_TOPK = 16

LANES = 128
SUBLANES = 8

Z_QKV = 0
Z_Z = Z_QKV + B_CONV_CH
Z_AQ = Z_Z + B_WIDTH
Z_AK = Z_AQ + A_WIDTH
Z_AV = Z_AK + A_KV_WIDTH
Z_AB = Z_AV + A_KV_WIDTH
Z_USED = Z_AB + 2 * B_HEADS
Z_WIDTH = 11264

V7X_VMEM_BYTES = 64 * 1024 * 1024
VMEM_LIMIT = V7X_VMEM_BYTES - 8 * 1024 * 1024

NT_DIMS = (((1,), (1,)), ((), ()))
TN_DIMS = (((0,), (0,)), ((), ()))


def _cparams(sem):
    return pltpu.CompilerParams(dimension_semantics=sem, vmem_limit_bytes=VMEM_LIMIT)


def _rms_rows(x, g):
    ms = jnp.mean(x * x, axis=-1, keepdims=True)
    return x * lax.rsqrt(ms + EPS) * g


def _norm_matmul_kernel(x_ref, g_ref, w_ref, o_ref, *rest, emit_hn):
    hn_ref = rest[-1]
    j = pl.program_id(1)

    @pl.when(j == 0)
    def _():
        hn = _rms_rows(x_ref[...], g_ref[...]).astype(BF16)
        hn_ref[...] = hn
        if emit_hn:
            rest[0][...] = hn.T

    o_ref[...] = jnp.dot(hn_ref[...], w_ref[...], preferred_element_type=F32)


def norm_matmul(x, g, w, *, tm, tn, emit_hn=False):
    t, d = x.shape
    n = w.shape[1]
    out_shape = [jax.ShapeDtypeStruct((t, n), F32)]
    out_specs = [pl.BlockSpec((tm, tn), lambda i, j: (i, j))]
    if emit_hn:
        out_shape.append(jax.ShapeDtypeStruct((d, t), BF16))
        out_specs.append(pl.BlockSpec((d, tm), lambda i, j: (0, i)))
    res = pl.pallas_call(
        functools.partial(_norm_matmul_kernel, emit_hn=emit_hn),
        grid=(t // tm, n // tn),
        in_specs=[pl.BlockSpec((tm, d), lambda i, j: (i, 0)),
                  pl.BlockSpec((1, d), lambda i, j: (0, 0)),
                  pl.BlockSpec((d, tn), lambda i, j: (0, j), pipeline_mode=pl.Buffered(1 if tn == n else 2))],
        out_specs=out_specs,
        out_shape=out_shape,
        scratch_shapes=[pltpu.VMEM((tm, d), BF16)],
        compiler_params=_cparams(("parallel", "arbitrary")),
        name="norm_matmul",
    )(x, g.reshape(1, d), w)
    return res if emit_hn else res[0]


def _mix_out_kernel(x_ref, a_ref, b_ref, wa_ref, wb_ref, o_ref):
    acc = jnp.dot(a_ref[...], wa_ref[...], preferred_element_type=F32)
    acc = acc + jnp.dot(b_ref[...], wb_ref[...], preferred_element_type=F32)
    o_ref[...] = x_ref[...] + acc


def mix_out_residual(x, a, b, w, *, tm, tn):
    t, n = x.shape
    ka, kb = a.shape[1], b.shape[1]
    assert ka == kb and w.shape[0] == ka + kb
    return pl.pallas_call(
        _mix_out_kernel,
        grid=(t // tm, n // tn),
        in_specs=[pl.BlockSpec((tm, tn), lambda i, j: (i, j)),
                  pl.BlockSpec((tm, ka), lambda i, j: (i, 0)),
                  pl.BlockSpec((tm, kb), lambda i, j: (i, 0)),
                  pl.BlockSpec((ka, tn), lambda i, j: (0, j)),
                  pl.BlockSpec((kb, tn), lambda i, j: (1, j))],
        out_specs=pl.BlockSpec((tm, tn), lambda i, j: (i, j)),
        out_shape=jax.ShapeDtypeStruct((t, n), F32),
        compiler_params=_cparams(("parallel", "arbitrary")),
        name="mix_out_residual",
    )(x, a, b, w, w)


def _rmsnorm_kernel(x_ref, g_ref, o_ref):
    o_ref[...] = _rms_rows(x_ref[...], g_ref[...])


def rmsnorm_rows(x, g, *, row0, n_rows, tm):
    d = x.shape[1]
    r0 = row0 // tm
    return pl.pallas_call(
        _rmsnorm_kernel,
        grid=(n_rows // tm,),
        in_specs=[pl.BlockSpec((tm, d), lambda i: (r0 + i, 0)),
                  pl.BlockSpec((1, d), lambda i: (0, 0))],
        out_specs=pl.BlockSpec((tm, d), lambda i: (i, 0)),
        out_shape=jax.ShapeDtypeStruct((n_rows, d), F32),
        compiler_params=_cparams(("parallel",)),
        name="final_rmsnorm",
    )(x, g.reshape(1, d))


ALIBI_SLOPES = [float(2.0 ** (-8.0 * (n + 1) / A_HEADS)) for n in range(A_HEADS)]
BAND = (WIN_CHUNKS + 1) * CHUNK


def _swa_kernel(sink_ref, q_ref, k0_ref, k1_ref, k2_ref, v0_ref, v1_ref, v2_ref, o_ref, *,
                chunks_per_seq):
    qi = lax.broadcasted_iota(jnp.int32, (CHUNK, BAND), 0)
    kj = lax.broadcasted_iota(jnp.int32, (CHUNK, BAND), 1)
    dist = jnp.abs(qi + WINDOW - kj).astype(F32)
    if chunks_per_seq is None:
        valid = None
    else:
        c = pl.program_id(0) % chunks_per_seq
        valid = kj >= (WIN_CHUNKS - jnp.minimum(c, WIN_CHUNKS)) * CHUNK
    scale = A_HEAD_DIM ** -0.5
    scores, vals = [], []
    for h in range(A_KV_HEADS):
        hs = slice(h * A_HEAD_DIM, (h + 1) * A_HEAD_DIM)
        qh = jnp.concatenate(
            [q_ref[:, (h * A_GROUP + g) * A_HEAD_DIM:(h * A_GROUP + g + 1) * A_HEAD_DIM]
             for g in range(A_GROUP)], axis=0).astype(BF16)
        kh = jnp.concatenate([k0_ref[:, hs], k1_ref[:, hs], k2_ref[:, hs]], axis=0).astype(BF16)
        vals.append(jnp.concatenate([v0_ref[:, hs], v1_ref[:, hs], v2_ref[:, hs]], axis=0).astype(BF16))
        scores.append(lax.dot_general(qh, kh, NT_DIMS, preferred_element_type=F32) * scale)
    probs = []
    for h in range(A_KV_HEADS):
        ps = []
        for g in range(A_GROUP):
            n = h * A_GROUP + g
            sg = scores[h][g * CHUNK:(g + 1) * CHUNK] - ALIBI_SLOPES[n] * dist
            if valid is not None:
                sg = jnp.where(valid, sg, -jnp.inf)
            sk = sink_ref[n]
            m = jnp.maximum(jnp.max(sg, axis=-1, keepdims=True), sk)
            p = jnp.exp(sg - m)
            p = p / (jnp.sum(p, axis=-1, keepdims=True) + jnp.exp(sk - m))
            ps.append(p.astype(BF16))
        probs.append(jnp.concatenate(ps, axis=0))
    for h in range(A_KV_HEADS):
        oh = jnp.dot(probs[h], vals[h], preferred_element_type=F32)
        for g in range(A_GROUP):
            n = h * A_GROUP + g
            o_ref[:, n * A_HEAD_DIM:(n + 1) * A_HEAD_DIM] = oh[g * CHUNK:(g + 1) * CHUNK].astype(BF16)


def swa_prompt(zz, sinks, n_seq, chunks_per_seq, total_rows):
    qb, kb, vb = Z_AQ // A_WIDTH, Z_AK // A_KV_WIDTH, Z_AV // A_KV_WIDTH

    def hist(j, col):
        def index(i):
            c = i % chunks_per_seq
            return (i - jnp.minimum(c, WIN_CHUNKS - j), col)
        return pl.BlockSpec((CHUNK, A_KV_WIDTH), index)

    return pl.pallas_call(
        functools.partial(_swa_kernel, chunks_per_seq=chunks_per_seq),
        grid=(n_seq * chunks_per_seq,),
        in_specs=[pl.BlockSpec(memory_space=pltpu.SMEM),
                  pl.BlockSpec((CHUNK, A_WIDTH), lambda i: (i, qb)),
                  hist(0, kb), hist(1, kb), hist(2, kb),
                  hist(0, vb), hist(1, vb), hist(2, vb)],
        out_specs=pl.BlockSpec((CHUNK, A_WIDTH), lambda i: (i, 0)),
        out_shape=jax.ShapeDtypeStruct((total_rows, A_WIDTH), BF16),
        compiler_params=_cparams(("parallel",)),
        name="swa_prompt",
    )(sinks, zz, zz, zz, zz, zz, zz, zz)


def swa_sample(a_out, zz, sinks, ck, cv, row_chunk0):
    n_seq = ck.shape[0] // WINDOW
    qb, kb, vb = Z_AQ // A_WIDTH, Z_AK // A_KV_WIDTH, Z_AV // A_KV_WIDTH
    cache0 = pl.BlockSpec((CHUNK, A_KV_WIDTH), lambda i: (WIN_CHUNKS * i, 0))
    cache1 = pl.BlockSpec((CHUNK, A_KV_WIDTH), lambda i: (WIN_CHUNKS * i + 1, 0))

    def body(sink_ref, q_ref, k0, k1, k2, v0, v1, v2, prev_ref, o_ref):
        del prev_ref
        _swa_kernel(sink_ref, q_ref, k0, k1, k2, v0, v1, v2, o_ref, chunks_per_seq=None)

    return pl.pallas_call(
        body,
        grid=(n_seq,),
        in_specs=[pl.BlockSpec(memory_space=pltpu.SMEM),
                  pl.BlockSpec((CHUNK, A_WIDTH), lambda i: (row_chunk0 + i, qb)),
                  cache0, cache1, pl.BlockSpec((CHUNK, A_KV_WIDTH), lambda i: (row_chunk0 + i, kb)),
                  cache0, cache1, pl.BlockSpec((CHUNK, A_KV_WIDTH), lambda i: (row_chunk0 + i, vb)),
                  pl.BlockSpec(memory_space=pl.ANY)],
        out_specs=pl.BlockSpec((CHUNK, A_WIDTH), lambda i: (row_chunk0 + i, 0)),
        out_shape=jax.ShapeDtypeStruct(a_out.shape, BF16),
        input_output_aliases={8: 0},
        compiler_params=_cparams(("parallel",)),
        name="swa_sample",
    )(sinks, zz, ck, ck, zz, cv, cv, zz, a_out)


CONV_ROWS = SUBLANES


def _silu(x):
    return x * jax.nn.sigmoid(x)


def _gdn_kernel(qkv_ref, z_ref, ab_ref, conv0_ref, s0_ref, cw_ref, alog_ref, dtb_ref, ng_ref,
                o_ref, convst_ref, s_ref, xext_ref, *, n_multi, chunks_per_seq):
    i = pl.program_id(0)
    in_multi = i < n_multi
    c = i % chunks_per_seq
    first = jnp.logical_or(jnp.logical_not(in_multi), c == 0)
    last = jnp.logical_or(jnp.logical_not(in_multi), c == chunks_per_seq - 1)

    @pl.when(first)
    def _():
        xext_ref[0:CONV_ROWS, :] = conv0_ref[0]
        s_ref[...] = s0_ref[...]

    xext_ref[CONV_ROWS:CONV_ROWS + CHUNK, :] = qkv_ref[...]

    ri = lax.broadcasted_iota(jnp.int32, (CHUNK, CHUNK), 0)
    ci = lax.broadcasted_iota(jnp.int32, (CHUNK, CHUNK), 1)
    tri_incl = ri >= ci
    tri_strict = ri > ci

    ab = ab_ref[...]
    x = ab + dtb_ref[...]
    softplus = jnp.maximum(x, 0.0) + jnp.log1p(jnp.exp(-jnp.abs(x)))
    gate = -jnp.exp(alog_ref[...]) * softplus
    beta_all = jax.nn.sigmoid(ab)
    gcum = jnp.dot(tri_incl.astype(F32), gate, preferred_element_type=F32,
                   precision=lax.Precision.HIGHEST)
    gcum_t = gcum.T
    egc = jnp.exp(gcum)
    glast = gcum[CHUNK - 1:CHUNK, :]
    etail = jnp.exp(glast - gcum)
    eglast = jnp.exp(glast)

    def conv(col0):
        cs = slice(col0, col0 + LANES)
        y = xext_ref[CONV_ROWS:CONV_ROWS + CHUNK, cs] * cw_ref[CONV_W - 1:CONV_W, cs]
        for j in range(1, CONV_W):
            y = y + xext_ref[CONV_ROWS - j:CONV_ROWS - j + CHUNK, cs] * cw_ref[CONV_W - 1 - j:CONV_W - j, cs]
        return _silu(y)

    def l2n(a):
        return a * lax.rsqrt(jnp.sum(a * a, axis=-1, keepdims=True) + EPS)

    heads = range(B_HEADS)
    qs, ks, attns, xss, ms = [], [], [], [], []
    for h in heads:
        q = l2n(conv(h * B_DK)) * (B_DK ** -0.5)
        k = l2n(conv(B_QK_WIDTH + h * B_DK))
        v = conv(2 * B_QK_WIDTH + h * B_DV)
        beta = beta_all[:, B_HEADS + h:B_HEADS + h + 1]
        gc_col = gcum[:, h:h + 1]
        gc_row = gcum_t[h:h + 1, :]
        decay = jnp.exp(jnp.where(tri_incl, gc_col - gc_row, -jnp.inf))
        kbeta = k * beta
        kk_qk = lax.dot_general(jnp.concatenate([kbeta, q], axis=0).astype(BF16), k.astype(BF16),
                                NT_DIMS, preferred_element_type=F32)
        ms.append(-jnp.where(tri_strict, kk_qk[:CHUNK] * decay, 0.0))
        attns.append((kk_qk[CHUNK:] * decay).astype(BF16))
        xss.append(jnp.concatenate([v * beta, kbeta * egc[:, h:h + 1]], axis=1))
        qs.append((q * egc[:, h:h + 1]).astype(BF16))
        ks.append((k * etail[:, h:h + 1]).astype(BF16))
    for _ in range(6):
        for h in heads:
            mb = ms[h].astype(BF16)
            y = jnp.dot(mb, jnp.concatenate([xss[h].astype(BF16), mb], axis=1), preferred_element_type=F32)
            xss[h] = xss[h] + y[:, :2 * B_DV]
            ms[h] = y[:, 2 * B_DV:]
    v_news, os_ = [], []
    for h in heads:
        u, w = xss[h][:, :B_DV], xss[h][:, B_DV:]
        ws_qs = jnp.dot(jnp.concatenate([w.astype(BF16), qs[h]], axis=0),
                        s_ref[0, h].astype(BF16), preferred_element_type=F32)
        v_news.append((u - ws_qs[:CHUNK]).astype(BF16))
        os_.append(ws_qs[CHUNK:])
    for h in heads:
        o = os_[h] + jnp.dot(attns[h], v_news[h], preferred_element_type=F32)
        s_ref[0, h] = s_ref[0, h] * eglast[:, h:h + 1] + lax.dot_general(
            ks[h], v_news[h], TN_DIMS, preferred_element_type=F32)
        zs = slice(h * B_DV, (h + 1) * B_DV)
        o_ref[:, zs] = (_rms_rows(o, ng_ref[...]) * _silu(z_ref[:, zs])).astype(BF16)

    xext_ref[0:CONV_ROWS, :] = xext_ref[CHUNK:CHUNK + CONV_ROWS, :]

    @pl.when(last)
    def _():
        convst_ref[0] = xext_ref[0:CONV_ROWS, :]


def gdn_mixer(zz, conv0, s0, conv_w, a_log, dt_bias, norm_g, *, n_multi_seq, chunks_per_seq):
    t = zz.shape[0]
    n_seq = s0.shape[0]
    n_multi = n_multi_seq * chunks_per_seq
    n_steps = t // CHUNK
    assert n_steps == n_multi + (n_seq - n_multi_seq)

    def seq(i):
        return jnp.where(i < n_multi, i // chunks_per_seq, i - n_multi + n_multi_seq)

    pad = lambda a: jnp.pad(a.astype(F32), (0, LANES - a.shape[0])).reshape(1, LANES)
    kern = functools.partial(_gdn_kernel, n_multi=n_multi, chunks_per_seq=chunks_per_seq)
    return pl.pallas_call(
        kern,
        grid=(n_steps,),
        in_specs=[pl.BlockSpec((CHUNK, B_CONV_CH), lambda i: (i, Z_QKV // B_CONV_CH)),
                  pl.BlockSpec((CHUNK, B_WIDTH), lambda i: (i, Z_Z // B_WIDTH)),
                  pl.BlockSpec((CHUNK, LANES), lambda i: (i, Z_AB // LANES)),
                  pl.BlockSpec((1, CONV_ROWS, B_CONV_CH), lambda i: (seq(i), 0, 0)),
                  pl.BlockSpec((1, B_HEADS, B_DK, B_DV), lambda i: (seq(i), 0, 0, 0)),
                  pl.BlockSpec((CONV_W, B_CONV_CH), lambda i: (0, 0)),
                  pl.BlockSpec((1, LANES), lambda i: (0, 0)),
                  pl.BlockSpec((1, LANES), lambda i: (0, 0)),
                  pl.BlockSpec((1, B_DV), lambda i: (0, 0))],
        out_specs=[pl.BlockSpec((CHUNK, B_WIDTH), lambda i: (i, 0)),
                   pl.BlockSpec((1, CONV_ROWS, B_CONV_CH), lambda i: (seq(i), 0, 0)),
                   pl.BlockSpec((1, B_HEADS, B_DK, B_DV), lambda i: (seq(i), 0, 0, 0))],
        out_shape=[jax.ShapeDtypeStruct((t, B_WIDTH), BF16),
                   jax.ShapeDtypeStruct((n_seq, CONV_ROWS, B_CONV_CH), F32),
                   jax.ShapeDtypeStruct((n_seq, B_HEADS, B_DK, B_DV), F32)],
        scratch_shapes=[pltpu.VMEM((CONV_ROWS + CHUNK, B_CONV_CH), F32)],
        compiler_params=_cparams(("arbitrary",)),
        name="gdn_mixer",
    )(zz, zz, zz, conv0, s0, conv_w, pad(a_log), pad(dt_bias), norm_g.reshape(1, B_DV))


def _cross_kernel(x_ref, g_ref, wq_ref, mk_ref, mv_ref, wo_ref, o_ref, *, n_batch):
    x = x_ref[...]
    tm = x.shape[0]
    rows = tm // n_batch
    hn = _rms_rows(x, g_ref[...]).astype(BF16)
    q = jnp.dot(hn, wq_ref[...], preferred_element_type=F32)
    scale = M_HEAD_DIM ** -0.5
    pairs = [(b, h) for b in range(n_batch) for h in range(M_HEADS)]
    scores, vals = {}, {}
    for b, h in pairs:
        hs = slice(h * M_HEAD_DIM, (h + 1) * M_HEAD_DIM)
        qh = q[b * rows:(b + 1) * rows, hs].astype(BF16)
        kh = mk_ref[b * N_MEM:(b + 1) * N_MEM, hs].astype(BF16)
        vals[b, h] = mv_ref[b * N_MEM:(b + 1) * N_MEM, hs].astype(BF16)
        scores[b, h] = lax.dot_general(qh, kh, NT_DIMS, preferred_element_type=F32) * scale
    probs = {}
    for b, h in pairs:
        s = scores[b, h]
        p = jnp.exp(s - jnp.max(s, axis=-1, keepdims=True))
        probs[b, h] = (p / jnp.sum(p, axis=-1, keepdims=True)).astype(BF16)
    outs = []
    for b in range(n_batch):
        outs.append(jnp.concatenate(
            [jnp.dot(probs[b, h], vals[b, h], preferred_element_type=F32) for h in range(M_HEADS)], axis=1))
    o = jnp.concatenate(outs, axis=0) if n_batch > 1 else outs[0]
    o_ref[...] = x + jnp.dot(o.astype(BF16), wo_ref[...], preferred_element_type=F32)


def cross_block(x, g, wq, mk, mv, mk_col, mv_col, wo, *, row0, n_rows, rows_per_batch, tm):
    t, d = x.shape
    if rows_per_batch >= tm:
        n_batch = 1
        tiles_per_batch = rows_per_batch // tm
        mem_index = lambda col: (lambda i: (i // tiles_per_batch, col))
    else:
        n_batch = tm // rows_per_batch
        mem_index = lambda col: (lambda i: (i, col))
    r0 = row0 // tm
    return pl.pallas_call(
        functools.partial(_cross_kernel, n_batch=n_batch),
        grid=(n_rows // tm,),
        in_specs=[pl.BlockSpec((tm, d), lambda i: (r0 + i, 0)),
                  pl.BlockSpec((1, d), lambda i: (0, 0)),
                  pl.BlockSpec((d, M_WIDTH), lambda i: (0, 0)),
                  pl.BlockSpec((n_batch * N_MEM, M_WIDTH), mem_index(mk_col)),
                  pl.BlockSpec((n_batch * N_MEM, M_WIDTH), mem_index(mv_col)),
                  pl.BlockSpec((M_WIDTH, d), lambda i: (0, 0))],
        out_specs=pl.BlockSpec((tm, d), lambda i: (r0 + i, 0)),
        out_shape=jax.ShapeDtypeStruct((t, d), F32),
        input_output_aliases={0: 0},
        compiler_params=_cparams(("parallel",)),
        name="cross_block",
    )(x, g.reshape(1, d), wq, mk, mv, wo)


def _batcher_network(n):
    pairs = []
    t = n.bit_length() - 1
    p = 1 << (t - 1)
    while p > 0:
        q, r, d = 1 << (t - 1), 0, p
        while d > 0:
            for i in range(n - d):
                if i & p == r:
                    pairs.append((i, i + d))
            d, q, r = q - p, q >> 1, p
        p >>= 1
    return pairs


SORT16 = _batcher_network(P_TOPK)


def _cmpx(vals, i, j):
    vals[i], vals[j] = jnp.maximum(vals[i], vals[j]), jnp.minimum(vals[i], vals[j])


def _sort_bitonic(vals):
    stride = P_TOPK // 2
    while stride:
        for i in range(P_TOPK):
            if (i // stride) % 2 == 0:
                _cmpx(vals, i, i + stride)
        stride //= 2
    return vals


def _merge_sublanes(vals):
    for shift in (4, 2, 1):
        other = [pltpu.roll(v, shift, axis=0) for v in vals]
        vals = _sort_bitonic([jnp.maximum(vals[i], other[P_TOPK - 1 - i]) for i in range(P_TOPK)])
    return vals


def _peer_select_kernel(q_ref, sk1_ref, sk2_ref, s1_ref, s2_ref, tau_ref, m_ref, iz_ref):
    tm = q_ref.shape[0]
    sub = lax.broadcasted_iota(jnp.int32, (SUBLANES, tm), 0)
    neg = jnp.full((SUBLANES, tm), -jnp.inf, F32)

    def top16(s):
        vals = [s[SUBLANES * j:SUBLANES * (j + 1)] for j in range(N_KEYS // SUBLANES)]
        for i, j in SORT16:
            _cmpx(vals, i, j)
        return _merge_sublanes(vals)

    def spread(vals):
        out = vals[0]
        for b in range(1, SUBLANES):
            out = jnp.where(sub == b, vals[b], out)
        return out

    for h in range(P_HEADS):
        q1 = q_ref[:, h * P_DKEY:h * P_DKEY + P_HALF].astype(BF16)
        q2 = q_ref[:, h * P_DKEY + P_HALF:(h + 1) * P_DKEY].astype(BF16)
        s1 = lax.dot_general(sk1_ref[h], q1, NT_DIMS, preferred_element_type=F32)
        s2 = lax.dot_general(sk2_ref[h], q2, NT_DIMS, preferred_element_type=F32)
        s1_ref[h] = s1
        s2_ref[h] = s2
        t1 = top16(s1)
        t2 = top16(s2)
        t2_lo, t2_hi = spread(t2[:SUBLANES]), spread(t2[SUBLANES:])
        cand = [jnp.where((a + 1) * (sub + 1) <= P_TOPK, t1[a] + t2_lo, neg) for a in range(P_TOPK)]
        cand[P_TOPK - 1] = jnp.maximum(cand[P_TOPK - 1], t1[0] + t2_hi)
        best = _merge_sublanes(_sort_bitonic(cand))
        m = best[0][0:1, :]
        z = jnp.zeros((1, tm), F32)
        for k in range(P_TOPK):
            z = z + jnp.exp(best[k][0:1, :] - m)
        tau_ref[h:h + 1, :] = best[P_TOPK - 1][0:1, :]
        m_ref[h:h + 1, :] = m
        iz_ref[h:h + 1, :] = 1.0 / z


def peer_select(q, sk1, sk2, *, tm):
    t = q.shape[0]
    big = jax.ShapeDtypeStruct((P_HEADS, N_KEYS, t), F32)
    small = jax.ShapeDtypeStruct((P_HEADS, t), F32)
    big_spec = pl.BlockSpec((P_HEADS, N_KEYS, tm), lambda i: (0, 0, i))
    small_spec = pl.BlockSpec((P_HEADS, tm), lambda i: (0, i))
    key_spec = pl.BlockSpec((P_HEADS, N_KEYS, P_HALF), lambda i: (0, 0, 0))
    return pl.pallas_call(
        _peer_select_kernel,
        grid=(t // tm,),
        in_specs=[pl.BlockSpec((tm, P_HEADS * P_DKEY), lambda i: (i, 0)), key_spec, key_spec],
        out_specs=[big_spec, big_spec, small_spec, small_spec, small_spec],
        out_shape=[big, big, small, small, small],
        compiler_params=_cparams(("parallel",)),
        name="peer_select",
    )(q, sk1, sk2)


SQRT_HALF = float(np.sqrt(0.5))


def _peer_main_kernel(hnT_ref, u_ref, vt_ref, s1_ref, s2_ref, tau_ref, m_ref, iz_ref, o_ref,
                      a0_ref, a1_ref, p_ref):
    e = pl.program_id(1)
    te, tm = a0_ref.shape
    rows = te // N_KEYS
    half = tm // 2
    prev_tile = jnp.maximum(e - 1, 0)

    @pl.when(e == 0)
    def _():
        o_ref[...] = jnp.zeros_like(o_ref)
        a1_ref[...] = jnp.zeros_like(a1_ref)

    def step(a_cur, a_prev):
        for hs in (slice(0, half), slice(half, tm)):
            a_cur[:, hs] = jnp.dot(u_ref[...], hnT_ref[:, hs], preferred_element_type=F32)
            for r in range(rows):
                i1 = prev_tile * rows + r
                rs = slice(r * N_KEYS, (r + 1) * N_KEYS)
                for c in range(hs.start // LANES, hs.stop // LANES):
                    cs = slice(c * LANES, (c + 1) * LANES)
                    g = jnp.zeros((N_KEYS, LANES), F32)
                    for h in range(P_HEADS):
                        s = s1_ref[i1, h:h + 1, cs] + s2_ref[h, :, cs]
                        w = jnp.exp(s - m_ref[h:h + 1, cs]) * iz_ref[h:h + 1, cs]
                        g = g + jnp.where(s >= tau_ref[h:h + 1, cs], w, 0.0)
                    a = a_prev[rs, cs]
                    act = 0.5 * a * (1.0 + lax.erf(a * SQRT_HALF))
                    p_ref[rs, cs] = (act * g).astype(BF16)
            o_ref[:, hs] += jnp.dot(vt_ref[...], p_ref[:, hs], preferred_element_type=F32)

    @pl.when(e % 2 == 0)
    def _():
        step(a0_ref, a1_ref)

    @pl.when(e % 2 == 1)
    def _():
        step(a1_ref, a0_ref)


def peer_main(hnT, u, vt, s1, s2, tau, m, iz, *, tm, te):
    d, t = hnT.shape
    n_e = N_EXPERTS // te
    small_spec = pl.BlockSpec((P_HEADS, tm), lambda i, e: (0, i))
    big_spec = pl.BlockSpec((P_HEADS, N_KEYS, tm), lambda i, e: (0, 0, i))
    s1_spec = pl.BlockSpec((N_KEYS, P_HEADS, tm), lambda i, e: (0, 0, i))
    return pl.pallas_call(
        _peer_main_kernel,
        grid=(t // tm, n_e + 1),
        in_specs=[pl.BlockSpec((d, tm), lambda i, e: (0, i)),
                  pl.BlockSpec((te, d), lambda i, e: (jnp.minimum(e, n_e - 1), 0)),
                  pl.BlockSpec((d, te), lambda i, e: (0, jnp.maximum(e - 1, 0))),
                  s1_spec, big_spec, small_spec, small_spec, small_spec],
        out_specs=pl.BlockSpec((d, tm), lambda i, e: (0, i)),
        out_shape=jax.ShapeDtypeStruct((d, t), F32),
        scratch_shapes=[pltpu.VMEM((te, tm), F32), pltpu.VMEM((te, tm), F32), pltpu.VMEM((te, tm), BF16)],
        compiler_params=_cparams(("parallel", "arbitrary")),
        name="peer_main",
    )(hnT, u, vt, jnp.transpose(s1, (1, 0, 2)), s2, tau, m, iz)


def _residual_t_kernel(x_ref, yt_ref, o_ref):
    o_ref[...] = x_ref[...] + yt_ref[...].T


def residual_add_transposed(x, yT, *, tm):
    t, d = x.shape
    return pl.pallas_call(
        _residual_t_kernel,
        grid=(t // tm,),
        in_specs=[pl.BlockSpec((tm, d), lambda i: (i, 0)),
                  pl.BlockSpec((d, tm), lambda i: (0, i))],
        out_specs=pl.BlockSpec((tm, d), lambda i: (i, 0)),
        out_shape=jax.ShapeDtypeStruct((t, d), F32),
        input_output_aliases={0: 0},
        compiler_params=_cparams(("parallel",)),
        name="residual_add_transposed",
    )(x, yT)


def kernel(x_prompt, x_sample, mem_prompt, cache_swa_k, cache_swa_v, state_conv, state_gdn, cache_mem_k, cache_mem_v, ln_mix, w_in, conv_w, a_log, dt_bias, gdn_norm, sinks, w_out, ln_cross, ln_mem, w_mq, w_mk, w_mv, w_mo, ln_ffn, w_pq, sub_keys1, sub_keys2, expert_u, expert_v, ln_final):
    Bp, Sp, D = x_prompt.shape
    Bs, Ts, _ = x_sample.shape
    assert Ts == CHUNK and Sp % CHUNK == 0 and cache_swa_k.shape[2] == WINDOW
    n_p = Bp * Sp
    n_s = Bs * Ts
    n_tok = n_p + n_s
    cps = Sp // CHUNK
    x = jnp.concatenate([x_prompt.reshape(n_p, D), x_sample.reshape(n_s, D)], axis=0)
    mem = mem_prompt.reshape(Bp * N_MEM, D)
    conv_pad = ((0, 0), (CONV_ROWS - (CONV_W - 1), 0), (0, 0))

    o_qkv = A_WIDTH + 2 * A_KV_WIDTH
    o_ab = o_qkv + B_CONV_CH
    o_z = o_ab + 2 * B_HEADS
    w_in_all = jnp.concatenate(
        [w_in[:, :, o_qkv:o_ab].astype(BF16), w_in[:, :, o_z:].astype(BF16), w_in[:, :, :o_qkv].astype(BF16),
         w_in[:, :, o_ab:o_z].astype(BF16), jnp.zeros((DEPTH, D, Z_WIDTH - Z_USED), BF16)], axis=2)

    outs = [[] for _ in range(10)]
    for l in range(DEPTH):
        w_in_l = w_in_all[l]
        w_out_l = w_out[l].astype(BF16)
        w_mq_l = w_mq[l].astype(BF16)
        w_mkv_l = jnp.concatenate([w_mk[l], w_mv[l]], axis=1).astype(BF16)
        w_mo_l = w_mo[l].astype(BF16)
        w_pq_l = w_pq[l].astype(BF16)
        u_l = expert_u[l].astype(BF16)
        vt_l = expert_v[l].astype(BF16).T
        sk1_l = sub_keys1[l].astype(BF16)
        sk2_l = sub_keys2[l].astype(BF16)

        zz = norm_matmul(x, ln_mix[l], w_in_l, tm=512, tn=1024)
        ck = cache_swa_k[l].reshape(Bs * WINDOW, A_KV_WIDTH)
        cv = cache_swa_v[l].reshape(Bs * WINDOW, A_KV_WIDTH)
        a_out = swa_prompt(zz, sinks[l], Bp, cps, n_tok)
        a_out = swa_sample(a_out, zz, sinks[l], ck, cv, n_p // CHUNK)
        conv0 = jnp.concatenate([jnp.zeros((Bp, CONV_ROWS, B_CONV_CH), F32),
                                 jnp.pad(state_conv[l], conv_pad)], axis=0)
        s0 = jnp.concatenate([jnp.zeros((Bp, B_HEADS, B_DK, B_DV), F32), state_gdn[l]], axis=0)
        b_out, convst, s_fin = gdn_mixer(zz, conv0, s0, conv_w[l], a_log[l], dt_bias[l], gdn_norm[l],
                                         n_multi_seq=Bp, chunks_per_seq=cps)
        x = mix_out_residual(x, a_out, b_out, w_out_l, tm=512, tn=1024)

        mkv = norm_matmul(mem, ln_mem[l], w_mkv_l, tm=512, tn=1024)
        x = cross_block(x, ln_cross[l], w_mq_l, mkv, mkv, 0, 1, w_mo_l,
                        row0=0, n_rows=n_p, rows_per_batch=Sp, tm=256)
        cmk = cache_mem_k[l].reshape(Bs * N_MEM, M_WIDTH)
        cmv = cache_mem_v[l].reshape(Bs * N_MEM, M_WIDTH)
        x = cross_block(x, ln_cross[l], w_mq_l, cmk, cmv, 0, 0, w_mo_l,
                        row0=n_p, n_rows=n_s, rows_per_batch=Ts, tm=256)

        qp, hnT = norm_matmul(x, ln_ffn[l], w_pq_l, tm=512, tn=P_HEADS * P_DKEY, emit_hn=True)
        s1, s2, tau, m, iz = peer_select(qp, sk1_l, sk2_l, tm=256)
        yT = peer_main(hnT, u_l, vt_l, s1, s2, tau, m, iz, tm=512, te=512)
        x = residual_add_transposed(x, yT, tm=256)

        kv_shape = (A_KV_HEADS, A_HEAD_DIM)
        kv_p = lax.slice(zz, (0, Z_AK), (n_p, Z_AB)).reshape(Bp, Sp, 2 * A_KV_WIDTH)[:, Sp - WINDOW:]
        kv_s = lax.slice(zz, (n_p, Z_AK), (n_tok, Z_AB)).reshape(Bs, Ts, 2 * A_KV_WIDTH)
        kb_p = kv_p[..., :A_KV_WIDTH].reshape((Bp, WINDOW) + kv_shape)
        vb_p = kv_p[..., A_KV_WIDTH:].reshape((Bp, WINDOW) + kv_shape)
        kb_s = jnp.concatenate([cache_swa_k[l][:, Ts:], kv_s[..., :A_KV_WIDTH].reshape((Bs, Ts) + kv_shape)], axis=1)
        vb_s = jnp.concatenate([cache_swa_v[l][:, Ts:], kv_s[..., A_KV_WIDTH:].reshape((Bs, Ts) + kv_shape)], axis=1)
        cst = convst[:, CONV_ROWS - (CONV_W - 1):]
        mk = mkv[:, :M_WIDTH].reshape(Bp, N_MEM, M_HEADS, M_HEAD_DIM)
        mv = mkv[:, M_WIDTH:].reshape(Bp, N_MEM, M_HEADS, M_HEAD_DIM)
        for lst, val in zip(outs, (kb_p, vb_p, cst[:Bp], s_fin[:Bp], mk, mv, kb_s, vb_s, cst[Bp:], s_fin[Bp:])):
            lst.append(val)

    y_prompt = rmsnorm_rows(x, ln_final, row0=0, n_rows=n_p, tm=512).reshape(Bp, Sp, D)
    y_sample = rmsnorm_rows(x, ln_final, row0=n_p, n_rows=n_s, tm=512).reshape(Bs, Ts, D)
    return (y_prompt, y_sample) + tuple(jnp.stack(o) for o in outs)
```

```python
import functools

import jax
import jax.numpy as jnp
import numpy as np
from jax import lax
from jax.experimental import pallas as pl
from jax.experimental.pallas import tpu as pltpu

F32 = jnp.float32
BF16 = jnp.bfloat16

D_MODEL = 4096
DEPTH = 4
CHUNK = 64
EPS = 1e-6
A_HEADS = 32
A_KV_HEADS = 4
A_GROUP = A_HEADS // A_KV_HEADS
A_HEAD_DIM = 64
WINDOW = 128
WIN_CHUNKS = WINDOW // CHUNK
A_WIDTH = A_HEADS * A_HEAD_DIM
A_KV_WIDTH = A_KV_HEADS * A_HEAD_DIM
B_HEADS = 16
B_DK = 128
B_DV = 128
CONV_W = 4
B_QK_WIDTH = B_HEADS * B_DK
B_WIDTH = B_HEADS * B_DV
B_CONV_CH = 2 * B_QK_WIDTH + B_WIDTH
N_MEM = 256
M_HEADS = 4
M_HEAD_DIM = 128
M_WIDTH = M_HEADS * M_HEAD_DIM
P_HEADS = 8
N_KEYS = 128
N_EXPERTS = N_KEYS * N_KEYS
P_DKEY = 256
P_HALF = P_DKEY // 2
P_TOPK = 16

LANES = 128
SUBLANES = 8

Z_QKV = 0
Z_Z = Z_QKV + B_CONV_CH
Z_AQ = Z_Z + B_WIDTH
Z_AK = Z_AQ + A_WIDTH
Z_AV = Z_AK + A_KV_WIDTH
Z_AB = Z_AV + A_KV_WIDTH
Z_USED = Z_AB + 2 * B_HEADS
Z_WIDTH = 11264

V7X_VMEM_BYTES = 64 * 1024 * 1024
VMEM_LIMIT = V7X_VMEM_BYTES - 4 * 1024 * 1024
MXU_WIDTH = 256

NT_DIMS = (((1,), (1,)), ((), ()))
TN_DIMS = (((0,), (0,)), ((), ()))


def _cparams(sem):
    return pltpu.CompilerParams(dimension_semantics=sem, vmem_limit_bytes=VMEM_LIMIT)


def _rms_rows(x, g):
    ms = jnp.mean(x * x, axis=-1, keepdims=True)
    return x * lax.rsqrt(ms + EPS) * g


def _norm_matmul_kernel(x_ref, g_ref, w_ref, o_ref, *rest, emit_hn):
    hn_ref = rest[-1]
    j = pl.program_id(1)

    @pl.when(j == 0)
    def _():
        hn = _rms_rows(x_ref[...], g_ref[...]).astype(BF16)
        hn_ref[...] = hn
        if emit_hn:
            rest[0][...] = hn.T

    o_ref[...] = jnp.dot(hn_ref[...], w_ref[...], preferred_element_type=F32)


def _layer_spec(block, index, layer, **kw):
    return pl.BlockSpec((None,) + block, lambda *ids: (layer,) + index(*ids), **kw)


def norm_matmul(x, g, w, *, layer, tm, tn, emit_hn=False):
    t, d = x.shape
    n = w.shape[2]
    out_shape = [jax.ShapeDtypeStruct((t, n), F32)]
    out_specs = [pl.BlockSpec((tm, tn), lambda i, j: (i, j))]
    if emit_hn:
        out_shape.append(jax.ShapeDtypeStruct((d, t), BF16))
        out_specs.append(pl.BlockSpec((d, tm), lambda i, j: (0, i)))
    res = pl.pallas_call(
        functools.partial(_norm_matmul_kernel, emit_hn=emit_hn),
        grid=(t // tm, n // tn),
        in_specs=[pl.BlockSpec((tm, d), lambda i, j: (i, 0)),
                  pl.BlockSpec((1, d), lambda i, j: (0, 0)),
                  _layer_spec((d, tn), lambda i, j: (0, j), layer,
                              pipeline_mode=pl.Buffered(1 if tn == n else 2))],
        out_specs=out_specs,
        out_shape=out_shape,
        scratch_shapes=[pltpu.VMEM((tm, d), BF16)],
        compiler_params=_cparams(("parallel", "arbitrary")),
        name="norm_matmul",
    )(x, g.reshape(1, d), w)
    return res if emit_hn else res[0]


def _mix_out_kernel(x_ref, a_ref, b_ref, wa_ref, wb_ref, o_ref):
    acc = jnp.dot(a_ref[...], wa_ref[...], preferred_element_type=F32)
    acc = acc + jnp.dot(b_ref[...], wb_ref[...], preferred_element_type=F32)
    o_ref[...] = x_ref[...] + acc


def mix_out_residual(x, a, b, w, *, layer, tm, tn):
    t, n = x.shape
    ka, kb = a.shape[1], b.shape[1]
    assert ka == kb and w.shape[1] == ka + kb
    return pl.pallas_call(
        _mix_out_kernel,
        grid=(t // tm, n // tn),
        in_specs=[pl.BlockSpec((tm, tn), lambda i, j: (i, j)),
                  pl.BlockSpec((tm, ka), lambda i, j: (i, 0)),
                  pl.BlockSpec((tm, kb), lambda i, j: (i, 0)),
                  _layer_spec((ka, tn), lambda i, j: (0, j), layer),
                  _layer_spec((kb, tn), lambda i, j: (1, j), layer)],
        out_specs=pl.BlockSpec((tm, tn), lambda i, j: (i, j)),
        out_shape=jax.ShapeDtypeStruct((t, n), F32),
        compiler_params=_cparams(("parallel", "arbitrary")),
        name="mix_out_residual",
    )(x, a, b, w, w)


def _rmsnorm_kernel(x_ref, g_ref, o_ref):
    o_ref[...] = _rms_rows(x_ref[...], g_ref[...])


def rmsnorm_rows(x, g, *, row0, n_rows, tm):
    d = x.shape[1]
    r0 = row0 // tm
    return pl.pallas_call(
        _rmsnorm_kernel,
        grid=(n_rows // tm,),
        in_specs=[pl.BlockSpec((tm, d), lambda i: (r0 + i, 0)),
                  pl.BlockSpec((1, d), lambda i: (0, 0))],
        out_specs=pl.BlockSpec((tm, d), lambda i: (i, 0)),
        out_shape=jax.ShapeDtypeStruct((n_rows, d), F32),
        compiler_params=_cparams(("parallel",)),
        name="final_rmsnorm",
    )(x, g.reshape(1, d))


ALIBI_SLOPES = [float(2.0 ** (-8.0 * (n + 1) / A_HEADS)) for n in range(A_HEADS)]
BAND = (WIN_CHUNKS + 1) * CHUNK


def _swa_kernel(sink_ref, q_ref, k0_ref, k1_ref, k2_ref, v0_ref, v1_ref, v2_ref, o_ref, *,
                chunks_per_seq):
    qi = lax.broadcasted_iota(jnp.int32, (CHUNK, BAND), 0)
    kj = lax.broadcasted_iota(jnp.int32, (CHUNK, BAND), 1)
    dist = jnp.abs(qi + WINDOW - kj).astype(F32)
    if chunks_per_seq is None:
        valid = None
    else:
        c = pl.program_id(0) % chunks_per_seq
        valid = kj >= (WIN_CHUNKS - jnp.minimum(c, WIN_CHUNKS)) * CHUNK
    scale = A_HEAD_DIM ** -0.5
    scores, vals = [], []
    for h in range(A_KV_HEADS):
        hs = slice(h * A_HEAD_DIM, (h + 1) * A_HEAD_DIM)
        qh = jnp.concatenate(
            [q_ref[:, (h * A_GROUP + g) * A_HEAD_DIM:(h * A_GROUP + g + 1) * A_HEAD_DIM]
             for g in range(A_GROUP)], axis=0).astype(BF16)
        kh = jnp.concatenate([k0_ref[:, hs], k1_ref[:, hs], k2_ref[:, hs]], axis=0).astype(BF16)
        vals.append(jnp.concatenate([v0_ref[:, hs], v1_ref[:, hs], v2_ref[:, hs]], axis=0).astype(BF16))
        scores.append(lax.dot_general(qh, kh, NT_DIMS, preferred_element_type=F32) * scale)
    probs = []
    for h in range(A_KV_HEADS):
        ps = []
        for g in range(A_GROUP):
            n = h * A_GROUP + g
            sg = scores[h][g * CHUNK:(g + 1) * CHUNK] - ALIBI_SLOPES[n] * dist
            if valid is not None:
                sg = jnp.where(valid, sg, -jnp.inf)
            sk = sink_ref[n]
            m = jnp.maximum(jnp.max(sg, axis=-1, keepdims=True), sk)
            p = jnp.exp(sg - m)
            p = p / (jnp.sum(p, axis=-1, keepdims=True) + jnp.exp(sk - m))
            ps.append(p.astype(BF16))
        probs.append(jnp.concatenate(ps, axis=0))
    for h in range(A_KV_HEADS):
        oh = jnp.dot(probs[h], vals[h], preferred_element_type=F32)
        for g in range(A_GROUP):
            n = h * A_GROUP + g
            o_ref[:, n * A_HEAD_DIM:(n + 1) * A_HEAD_DIM] = oh[g * CHUNK:(g + 1) * CHUNK].astype(BF16)


def swa_prompt(zz, sinks, n_seq, chunks_per_seq, total_rows):
    qb, kb, vb = Z_AQ // A_WIDTH, Z_AK // A_KV_WIDTH, Z_AV // A_KV_WIDTH

    def hist(j, col):
        def index(i):
            c = i % chunks_per_seq
            return (i - jnp.minimum(c, WIN_CHUNKS - j), col)
        return pl.BlockSpec((CHUNK, A_KV_WIDTH), index)

    return pl.pallas_call(
        functools.partial(_swa_kernel, chunks_per_seq=chunks_per_seq),
        grid=(n_seq * chunks_per_seq,),
        in_specs=[pl.BlockSpec(memory_space=pltpu.SMEM),
                  pl.BlockSpec((CHUNK, A_WIDTH), lambda i: (i, qb)),
                  hist(0, kb), hist(1, kb), hist(2, kb),
                  hist(0, vb), hist(1, vb), hist(2, vb)],
        out_specs=pl.BlockSpec((CHUNK, A_WIDTH), lambda i: (i, 0)),
        out_shape=jax.ShapeDtypeStruct((total_rows, A_WIDTH), BF16),
        compiler_params=_cparams(("parallel",)),
        name="swa_prompt",
    )(sinks, zz, zz, zz, zz, zz, zz, zz)


def swa_sample(a_out, zz, sinks, ck, cv, row_chunk0):
    n_seq = ck.shape[0] // WINDOW
    qb, kb, vb = Z_AQ // A_WIDTH, Z_AK // A_KV_WIDTH, Z_AV // A_KV_WIDTH
    cache0 = pl.BlockSpec((CHUNK, A_KV_WIDTH), lambda i: (WIN_CHUNKS * i, 0))
    cache1 = pl.BlockSpec((CHUNK, A_KV_WIDTH), lambda i: (WIN_CHUNKS * i + 1, 0))

    def body(sink_ref, q_ref, k0, k1, k2, v0, v1, v2, prev_ref, o_ref):
        del prev_ref
        _swa_kernel(sink_ref, q_ref, k0, k1, k2, v0, v1, v2, o_ref, chunks_per_seq=None)

    return pl.pallas_call(
        body,
        grid=(n_seq,),
        in_specs=[pl.BlockSpec(memory_space=pltpu.SMEM),
                  pl.BlockSpec((CHUNK, A_WIDTH), lambda i: (row_chunk0 + i, qb)),
                  cache0, cache1, pl.BlockSpec((CHUNK, A_KV_WIDTH), lambda i: (row_chunk0 + i, kb)),
                  cache0, cache1, pl.BlockSpec((CHUNK, A_KV_WIDTH), lambda i: (row_chunk0 + i, vb)),
                  pl.BlockSpec(memory_space=pl.ANY)],
        out_specs=pl.BlockSpec((CHUNK, A_WIDTH), lambda i: (row_chunk0 + i, 0)),
        out_shape=jax.ShapeDtypeStruct(a_out.shape, BF16),
        input_output_aliases={8: 0},
        compiler_params=_cparams(("parallel",)),
        name="swa_sample",
    )(sinks, zz, ck, ck, zz, cv, cv, zz, a_out)


CONV_ROWS = SUBLANES


def _silu(x):
    return x * jax.nn.sigmoid(x)


def _gdn_kernel(qkv_ref, z_ref, ab_ref, conv0_ref, s0_ref, cw_ref, alog_ref, dtb_ref, ng_ref,
                o_ref, convst_ref, s_ref, xext_ref, *, n_multi, chunks_per_seq):
    i = pl.program_id(0)
    in_multi = i < n_multi
    c = i % chunks_per_seq
    first = jnp.logical_or(jnp.logical_not(in_multi), c == 0)
    last = jnp.logical_or(jnp.logical_not(in_multi), c == chunks_per_seq - 1)

    @pl.when(first)
    def _():
        xext_ref[0:CONV_ROWS, :] = conv0_ref[0]
        s_ref[...] = s0_ref[...]

    xext_ref[CONV_ROWS:CONV_ROWS + CHUNK, :] = qkv_ref[...]

    ri = lax.broadcasted_iota(jnp.int32, (CHUNK, CHUNK), 0)
    ci = lax.broadcasted_iota(jnp.int32, (CHUNK, CHUNK), 1)
    tri_incl = ri >= ci
    tri_strict = ri > ci

    ab = ab_ref[...]
    x = ab + dtb_ref[...]
    softplus = jnp.maximum(x, 0.0) + jnp.log1p(jnp.exp(-jnp.abs(x)))
    gate = -jnp.exp(alog_ref[...]) * softplus
    beta_all = jax.nn.sigmoid(ab)
    gcum = jnp.dot(tri_incl.astype(F32), gate, preferred_element_type=F32,
                   precision=lax.Precision.HIGHEST)
    gcum_t = gcum.T
    egc = jnp.exp(gcum)
    glast = gcum[CHUNK - 1:CHUNK, :]
    etail = jnp.exp(glast - gcum)
    eglast = jnp.exp(glast)

    def conv(col0):
        cs = slice(col0, col0 + LANES)
        y = xext_ref[CONV_ROWS:CONV_ROWS + CHUNK, cs] * cw_ref[CONV_W - 1:CONV_W, cs]
        for j in range(1, CONV_W):
            y = y + xext_ref[CONV_ROWS - j:CONV_ROWS - j + CHUNK, cs] * cw_ref[CONV_W - 1 - j:CONV_W - j, cs]
        return _silu(y)

    def l2n(a):
        return a * lax.rsqrt(jnp.sum(a * a, axis=-1, keepdims=True) + EPS)

    heads = range(B_HEADS)
    qs, ks, attns, xss, ms = [], [], [], [], []
    for h in heads:
        q = l2n(conv(h * B_DK)) * (B_DK ** -0.5)
        k = l2n(conv(B_QK_WIDTH + h * B_DK))
        v = conv(2 * B_QK_WIDTH + h * B_DV)
        beta = beta_all[:, B_HEADS + h:B_HEADS + h + 1]
        gc_col = gcum[:, h:h + 1]
        gc_row = gcum_t[h:h + 1, :]
        decay = jnp.exp(jnp.where(tri_incl, gc_col - gc_row, -jnp.inf))
        kbeta = k * beta
        kk_qk = lax.dot_general(jnp.concatenate([kbeta, q], axis=0).astype(BF16), k.astype(BF16),
                                NT_DIMS, preferred_element_type=F32)
        ms.append(-jnp.where(tri_strict, kk_qk[:CHUNK] * decay, 0.0))
        attns.append((kk_qk[CHUNK:] * decay).astype(BF16))
        xss.append(jnp.concatenate([v * beta, kbeta * egc[:, h:h + 1]], axis=1))
        qs.append((q * egc[:, h:h + 1]).astype(BF16))
        ks.append((k * etail[:, h:h + 1]).astype(BF16))
    for _ in range(6):
        for h in heads:
            mb = ms[h].astype(BF16)
            y = jnp.dot(mb, jnp.concatenate([xss[h].astype(BF16), mb], axis=1), preferred_element_type=F32)
            xss[h] = xss[h] + y[:, :2 * B_DV]
            ms[h] = y[:, 2 * B_DV:]
    v_news, os_ = [], []
    for h in heads:
        u, w = xss[h][:, :B_DV], xss[h][:, B_DV:]
        ws_qs = jnp.dot(jnp.concatenate([w.astype(BF16), qs[h]], axis=0),
                        s_ref[0, h].astype(BF16), preferred_element_type=F32)
        v_news.append((u - ws_qs[:CHUNK]).astype(BF16))
        os_.append(ws_qs[CHUNK:])
    for h in heads:
        o = os_[h] + jnp.dot(attns[h], v_news[h], preferred_element_type=F32)
        s_ref[0, h] = s_ref[0, h] * eglast[:, h:h + 1] + lax.dot_general(
            ks[h], v_news[h], TN_DIMS, preferred_element_type=F32)
        zs = slice(h * B_DV, (h + 1) * B_DV)
        o_ref[:, zs] = (_rms_rows(o, ng_ref[...]) * _silu(z_ref[:, zs])).astype(BF16)

    xext_ref[0:CONV_ROWS, :] = xext_ref[CHUNK:CHUNK + CONV_ROWS, :]

    @pl.when(last)
    def _():
        convst_ref[0] = xext_ref[0:CONV_ROWS, :]


def gdn_mixer(zz, conv0, s0, conv_w, a_log, dt_bias, norm_g, *, n_multi_seq, chunks_per_seq):
    t = zz.shape[0]
    n_seq = s0.shape[0]
    n_multi = n_multi_seq * chunks_per_seq
    n_steps = t // CHUNK
    assert n_steps == n_multi + (n_seq - n_multi_seq)

    def seq(i):
        return jnp.where(i < n_multi, i // chunks_per_seq, i - n_multi + n_multi_seq)

    pad = lambda a: jnp.pad(a.astype(F32), (0, LANES - a.shape[0])).reshape(1, LANES)
    kern = functools.partial(_gdn_kernel, n_multi=n_multi, chunks_per_seq=chunks_per_seq)
    return pl.pallas_call(
        kern,
        grid=(n_steps,),
        in_specs=[pl.BlockSpec((CHUNK, B_CONV_CH), lambda i: (i, Z_QKV // B_CONV_CH)),
                  pl.BlockSpec((CHUNK, B_WIDTH), lambda i: (i, Z_Z // B_WIDTH)),
                  pl.BlockSpec((CHUNK, LANES), lambda i: (i, Z_AB // LANES)),
                  pl.BlockSpec((1, CONV_ROWS, B_CONV_CH), lambda i: (seq(i), 0, 0)),
                  pl.BlockSpec((1, B_HEADS, B_DK, B_DV), lambda i: (seq(i), 0, 0, 0)),
                  pl.BlockSpec((CONV_W, B_CONV_CH), lambda i: (0, 0)),
                  pl.BlockSpec((1, LANES), lambda i: (0, 0)),
                  pl.BlockSpec((1, LANES), lambda i: (0, 0)),
                  pl.BlockSpec((1, B_DV), lambda i: (0, 0))],
        out_specs=[pl.BlockSpec((CHUNK, B_WIDTH), lambda i: (i, 0)),
                   pl.BlockSpec((1, CONV_ROWS, B_CONV_CH), lambda i: (seq(i), 0, 0)),
                   pl.BlockSpec((1, B_HEADS, B_DK, B_DV), lambda i: (seq(i), 0, 0, 0))],
        out_shape=[jax.ShapeDtypeStruct((t, B_WIDTH), BF16),
                   jax.ShapeDtypeStruct((n_seq, CONV_ROWS, B_CONV_CH), F32),
                   jax.ShapeDtypeStruct((n_seq, B_HEADS, B_DK, B_DV), F32)],
        scratch_shapes=[pltpu.VMEM((CONV_ROWS + CHUNK, B_CONV_CH), F32)],
        compiler_params=_cparams(("arbitrary",)),
        name="gdn_mixer",
    )(zz, zz, zz, conv0, s0, conv_w, pad(a_log), pad(dt_bias), norm_g.reshape(1, B_DV))


def _cross_kernel(x_ref, g_ref, wq_ref, mk_ref, mv_ref, wo_ref, o_ref, *, n_batch):
    x = x_ref[...]
    tm = x.shape[0]
    rows = tm // n_batch
    hn = _rms_rows(x, g_ref[...]).astype(BF16)
    q = jnp.dot(hn, wq_ref[...], preferred_element_type=F32)
    scale = M_HEAD_DIM ** -0.5
    pairs = [(b, h) for b in range(n_batch) for h in range(M_HEADS)]
    scores, vals = {}, {}
    for b, h in pairs:
        hs = slice(h * M_HEAD_DIM, (h + 1) * M_HEAD_DIM)
        qh = q[b * rows:(b + 1) * rows, hs].astype(BF16)
        kh = mk_ref[b * N_MEM:(b + 1) * N_MEM, hs].astype(BF16)
        vals[b, h] = mv_ref[b * N_MEM:(b + 1) * N_MEM, hs].astype(BF16)
        scores[b, h] = lax.dot_general(qh, kh, NT_DIMS, preferred_element_type=F32) * scale
    probs = {}
    for b, h in pairs:
        s = scores[b, h]
        p = jnp.exp(s - jnp.max(s, axis=-1, keepdims=True))
        probs[b, h] = (p / jnp.sum(p, axis=-1, keepdims=True)).astype(BF16)
    outs = []
    for b in range(n_batch):
        outs.append(jnp.concatenate(
            [jnp.dot(probs[b, h], vals[b, h], preferred_element_type=F32) for h in range(M_HEADS)], axis=1))
    o = jnp.concatenate(outs, axis=0) if n_batch > 1 else outs[0]
    o_ref[...] = x + jnp.dot(o.astype(BF16), wo_ref[...], preferred_element_type=F32)


def cross_block(x, g, wq, mk, mv, mk_col, mv_col, wo, *, layer, row0, n_rows, rows_per_batch, tm):
    t, d = x.shape
    if rows_per_batch >= tm:
        n_batch = 1
        tiles_per_batch = rows_per_batch // tm
        mem_index = lambda col: (lambda i: (i // tiles_per_batch, col))
    else:
        n_batch = tm // rows_per_batch
        mem_index = lambda col: (lambda i: (i, col))
    r0 = row0 // tm
    return pl.pallas_call(
        functools.partial(_cross_kernel, n_batch=n_batch),
        grid=(n_rows // tm,),
        in_specs=[pl.BlockSpec((tm, d), lambda i: (r0 + i, 0)),
                  pl.BlockSpec((1, d), lambda i: (0, 0)),
                  _layer_spec((d, M_WIDTH), lambda i: (0, 0), layer),
                  pl.BlockSpec((n_batch * N_MEM, M_WIDTH), mem_index(mk_col)),
                  pl.BlockSpec((n_batch * N_MEM, M_WIDTH), mem_index(mv_col)),
                  _layer_spec((M_WIDTH, d), lambda i: (0, 0), layer)],
        out_specs=pl.BlockSpec((tm, d), lambda i: (r0 + i, 0)),
        out_shape=jax.ShapeDtypeStruct((t, d), F32),
        input_output_aliases={0: 0},
        compiler_params=_cparams(("parallel",)),
        name="cross_block",
    )(x, g.reshape(1, d), wq, mk, mv, wo)


def _batcher_network(n):
    pairs = []
    t = n.bit_length() - 1
    p = 1 << (t - 1)
    while p > 0:
        q, r, d = 1 << (t - 1), 0, p
        while d > 0:
            for i in range(n - d):
                if i & p == r:
                    pairs.append((i, i + d))
            d, q, r = q - p, q >> 1, p
        p >>= 1
    return pairs


SORT16 = _batcher_network(P_TOPK)


def _cmpx(vals, i, j):
    vals[i], vals[j] = jnp.maximum(vals[i], vals[j]), jnp.minimum(vals[i], vals[j])


def _sort_bitonic(vals):
    stride = P_TOPK // 2
    while stride:
        for i in range(P_TOPK):
            if (i // stride) % 2 == 0:
                _cmpx(vals, i, i + stride)
        stride //= 2
    return vals


def _merge_sublanes(vals):
    for shift in (4, 2, 1):
        other = [pltpu.roll(v, shift, axis=0) for v in vals]
        vals = _sort_bitonic([jnp.maximum(vals[i], other[P_TOPK - 1 - i]) for i in range(P_TOPK)])
    return vals


def _peer_select_kernel(q_ref, sk1_ref, sk2_ref, s1_ref, s2_ref, tau_ref, m_ref, iz_ref):
    tm = q_ref.shape[0]
    sub = lax.broadcasted_iota(jnp.int32, (SUBLANES, tm), 0)
    neg = jnp.full((SUBLANES, tm), -jnp.inf, F32)

    def top16(s):
        vals = [s[SUBLANES * j:SUBLANES * (j + 1)] for j in range(N_KEYS // SUBLANES)]
        for i, j in SORT16:
            _cmpx(vals, i, j)
        return _merge_sublanes(vals)

    def spread(vals):
        out = vals[0]
        for b in range(1, SUBLANES):
            out = jnp.where(sub == b, vals[b], out)
        return out

    for h in range(P_HEADS):
        q1 = q_ref[:, h * P_DKEY:h * P_DKEY + P_HALF].astype(BF16)
        q2 = q_ref[:, h * P_DKEY + P_HALF:(h + 1) * P_DKEY].astype(BF16)
        s1 = lax.dot_general(sk1_ref[h], q1, NT_DIMS, preferred_element_type=F32)
        s2 = lax.dot_general(sk2_ref[h], q2, NT_DIMS, preferred_element_type=F32)
        s1_ref[h] = s1
        s2_ref[h] = s2
        t1 = top16(s1)
        t2 = top16(s2)
        t2_lo, t2_hi = spread(t2[:SUBLANES]), spread(t2[SUBLANES:])
        cand = [jnp.where((a + 1) * (sub + 1) <= P_TOPK, t1[a] + t2_lo, neg) for a in range(P_TOPK)]
        cand[P_TOPK - 1] = jnp.maximum(cand[P_TOPK - 1], t1[0] + t2_hi)
        best = _merge_sublanes(_sort_bitonic(cand))
        m = best[0][0:1, :]
        z = jnp.zeros((1, tm), F32)
        for k in range(P_TOPK):
            z = z + jnp.exp(best[k][0:1, :] - m)
        tau_ref[h:h + 1, :] = best[P_TOPK - 1][0:1, :]
        m_ref[h:h + 1, :] = m
        iz_ref[h:h + 1, :] = 1.0 / z


def peer_select(q, sk1, sk2, *, tm):
    t = q.shape[0]
    big = jax.ShapeDtypeStruct((P_HEADS, N_KEYS, t), F32)
    small = jax.ShapeDtypeStruct((P_HEADS, t), F32)
    big_spec = pl.BlockSpec((P_HEADS, N_KEYS, tm), lambda i: (0, 0, i))
    small_spec = pl.BlockSpec((P_HEADS, tm), lambda i: (0, i))
    key_spec = pl.BlockSpec((P_HEADS, N_KEYS, P_HALF), lambda i: (0, 0, 0))
    return pl.pallas_call(
        _peer_select_kernel,
        grid=(t // tm,),
        in_specs=[pl.BlockSpec((tm, P_HEADS * P_DKEY), lambda i: (i, 0)), key_spec, key_spec],
        out_specs=[big_spec, big_spec, small_spec, small_spec, small_spec],
        out_shape=[big, big, small, small, small],
        compiler_params=_cparams(("parallel",)),
        name="peer_select",
    )(q, sk1, sk2)


SQRT_HALF = float(np.sqrt(0.5))


def _peer_main_kernel(hnT_ref, u_ref, vt_ref, s1_ref, s2_ref, tau_ref, m_ref, iz_ref, o_ref,
                      a0_ref, a1_ref, p_ref):
    e = pl.program_id(1)
    te, tm = a0_ref.shape
    rows = te // N_KEYS
    pieces = [slice(t0, t0 + MXU_WIDTH) for t0 in range(0, tm, MXU_WIDTH)]
    prev_tile = jnp.maximum(e - 1, 0)

    @pl.when(e == 0)
    def _():
        o_ref[...] = jnp.zeros_like(o_ref)
        a1_ref[...] = jnp.zeros_like(a1_ref)

    def step(a_cur, a_prev):
        for hs in pieces:
            a_cur[:, hs] = jnp.dot(u_ref[...], hnT_ref[:, hs], preferred_element_type=F32)
            for r in range(rows):
                i1 = prev_tile * rows + r
                rs = slice(r * N_KEYS, (r + 1) * N_KEYS)
                for c in range(hs.start // LANES, hs.stop // LANES):
                    cs = slice(c * LANES, (c + 1) * LANES)
                    g = jnp.zeros((N_KEYS, LANES), F32)
                    for h in range(P_HEADS):
                        s = s1_ref[i1, h:h + 1, cs] + s2_ref[h, :, cs]
                        w = jnp.exp(s - m_ref[h:h + 1, cs]) * iz_ref[h:h + 1, cs]
                        g = g + jnp.where(s >= tau_ref[h:h + 1, cs], w, 0.0)
                    a = a_prev[rs, cs]
                    act = 0.5 * a * (1.0 + lax.erf(a * SQRT_HALF))
                    p_ref[rs, cs] = (act * g).astype(BF16)
            o_ref[:, hs] += jnp.dot(vt_ref[...], p_ref[:, hs], preferred_element_type=F32)

    @pl.when(e % 2 == 0)
    def _():
        step(a0_ref, a1_ref)

    @pl.when(e % 2 == 1)
    def _():
        step(a1_ref, a0_ref)


def peer_main(hnT, u, vt, s1, s2, tau, m, iz, *, layer, tm, te):
    d, t = hnT.shape
    n_e = N_EXPERTS // te
    once = pl.Buffered(1)
    small_spec = pl.BlockSpec((P_HEADS, tm), lambda i, e: (0, i))
    big_spec = pl.BlockSpec((P_HEADS, N_KEYS, tm), lambda i, e: (0, 0, i), pipeline_mode=once)
    s1_spec = pl.BlockSpec((N_KEYS, P_HEADS, tm), lambda i, e: (0, 0, i), pipeline_mode=once)
    return pl.pallas_call(
        _peer_main_kernel,
        grid=(t // tm, n_e + 1),
        in_specs=[pl.BlockSpec((d, tm), lambda i, e: (0, i), pipeline_mode=once),
                  _layer_spec((te, d), lambda i, e: (jnp.minimum(e, n_e - 1), 0), layer),
                  _layer_spec((d, te), lambda i, e: (0, jnp.maximum(e - 1, 0)), layer),
                  s1_spec, big_spec, small_spec, small_spec, small_spec],
        out_specs=pl.BlockSpec((d, tm), lambda i, e: (0, i)),
        out_shape=jax.ShapeDtypeStruct((d, t), F32),
        scratch_shapes=[pltpu.VMEM((te, tm), F32), pltpu.VMEM((te, tm), F32), pltpu.VMEM((te, tm), BF16)],
        compiler_params=_cparams(("parallel", "arbitrary")),
        name="peer_main",
    )(hnT, u, vt, jnp.transpose(s1, (1, 0, 2)), s2, tau, m, iz)


def _residual_t_kernel(x_ref, yt_ref, o_ref):
    o_ref[...] = x_ref[...] + yt_ref[...].T


def residual_add_transposed(x, yT, *, tm):
    t, d = x.shape
    return pl.pallas_call(
        _residual_t_kernel,
        grid=(t // tm,),
        in_specs=[pl.BlockSpec((tm, d), lambda i: (i, 0)),
                  pl.BlockSpec((d, tm), lambda i: (0, i))],
        out_specs=pl.BlockSpec((tm, d), lambda i: (i, 0)),
        out_shape=jax.ShapeDtypeStruct((t, d), F32),
        input_output_aliases={0: 0},
        compiler_params=_cparams(("parallel",)),
        name="residual_add_transposed",
    )(x, yT)


def kernel(x_prompt, x_sample, mem_prompt, cache_swa_k, cache_swa_v, state_conv, state_gdn, cache_mem_k, cache_mem_v, ln_mix, w_in, conv_w, a_log, dt_bias, gdn_norm, sinks, w_out, ln_cross, ln_mem, w_mq, w_mk, w_mv, w_mo, ln_ffn, w_pq, sub_keys1, sub_keys2, expert_u, expert_v, ln_final):
    Bp, Sp, D = x_prompt.shape
    Bs, Ts, _ = x_sample.shape
    assert Ts == CHUNK and Sp % CHUNK == 0 and cache_swa_k.shape[2] == WINDOW
    n_p = Bp * Sp
    n_s = Bs * Ts
    n_tok = n_p + n_s
    cps = Sp // CHUNK
    x = jnp.concatenate([x_prompt.reshape(n_p, D), x_sample.reshape(n_s, D)], axis=0)
    mem = mem_prompt.reshape(Bp * N_MEM, D)
    conv_pad = ((0, 0), (CONV_ROWS - (CONV_W - 1), 0), (0, 0))

    o_qkv = A_WIDTH + 2 * A_KV_WIDTH
    o_ab = o_qkv + B_CONV_CH
    o_z = o_ab + 2 * B_HEADS
    w_in_all = jnp.concatenate(
        [w_in[:, :, o_qkv:o_ab].astype(BF16), w_in[:, :, o_z:].astype(BF16), w_in[:, :, :o_qkv].astype(BF16),
         w_in[:, :, o_ab:o_z].astype(BF16), jnp.zeros((DEPTH, D, Z_WIDTH - Z_USED), BF16)], axis=2)
    w_out_all = w_out.astype(BF16)
    w_mq_all = w_mq.astype(BF16)
    w_mkv_all = jnp.concatenate([w_mk, w_mv], axis=2).astype(BF16)
    w_mo_all = w_mo.astype(BF16)
    w_pq_all = w_pq.astype(BF16)
    u_all = expert_u.astype(BF16)
    vt_all = jnp.swapaxes(expert_v.astype(BF16), 1, 2)

    outs = [[] for _ in range(10)]
    for l in range(DEPTH):
        sk1_l = sub_keys1[l].astype(BF16)
        sk2_l = sub_keys2[l].astype(BF16)

        zz = norm_matmul(x, ln_mix[l], w_in_all, layer=l, tm=512, tn=1024)
        ck = cache_swa_k[l].reshape(Bs * WINDOW, A_KV_WIDTH)
        cv = cache_swa_v[l].reshape(Bs * WINDOW, A_KV_WIDTH)
        a_out = swa_prompt(zz, sinks[l], Bp, cps, n_tok)
        a_out = swa_sample(a_out, zz, sinks[l], ck, cv, n_p // CHUNK)
        conv0 = jnp.concatenate([jnp.zeros((Bp, CONV_ROWS, B_CONV_CH), F32),
                                 jnp.pad(state_conv[l], conv_pad)], axis=0)
        s0 = jnp.concatenate([jnp.zeros((Bp, B_HEADS, B_DK, B_DV), F32), state_gdn[l]], axis=0)
        b_out, convst, s_fin = gdn_mixer(zz, conv0, s0, conv_w[l], a_log[l], dt_bias[l], gdn_norm[l],
                                         n_multi_seq=Bp, chunks_per_seq=cps)
        x = mix_out_residual(x, a_out, b_out, w_out_all, layer=l, tm=512, tn=1024)

        mkv = norm_matmul(mem, ln_mem[l], w_mkv_all, layer=l, tm=512, tn=2 * M_WIDTH)
        x = cross_block(x, ln_cross[l], w_mq_all, mkv, mkv, 0, 1, w_mo_all,
                        layer=l, row0=0, n_rows=n_p, rows_per_batch=Sp, tm=256)
        cmk = cache_mem_k[l].reshape(Bs * N_MEM, M_WIDTH)
        cmv = cache_mem_v[l].reshape(Bs * N_MEM, M_WIDTH)
        x = cross_block(x, ln_cross[l], w_mq_all, cmk, cmv, 0, 0, w_mo_all,
                        layer=l, row0=n_p, n_rows=n_s, rows_per_batch=Ts, tm=256)

        qp, hnT = norm_matmul(x, ln_ffn[l], w_pq_all, layer=l, tm=512, tn=P_HEADS * P_DKEY, emit_hn=True)
        s1, s2, tau, m, iz = peer_select(qp, sk1_l, sk2_l, tm=256)
        yT = peer_main(hnT, u_all, vt_all, s1, s2, tau, m, iz, layer=l, tm=768, te=512)
        x = residual_add_transposed(x, yT, tm=256)

        kv_shape = (A_KV_HEADS, A_HEAD_DIM)
        kv_p = lax.slice(zz, (0, Z_AK), (n_p, Z_AB)).reshape(Bp, Sp, 2 * A_KV_WIDTH)[:, Sp - WINDOW:]
        kv_s = lax.slice(zz, (n_p, Z_AK), (n_tok, Z_AB)).reshape(Bs, Ts, 2 * A_KV_WIDTH)
        kb_p = kv_p[..., :A_KV_WIDTH].reshape((Bp, WINDOW) + kv_shape)
        vb_p = kv_p[..., A_KV_WIDTH:].reshape((Bp, WINDOW) + kv_shape)
        kb_s = jnp.concatenate([cache_swa_k[l][:, Ts:], kv_s[..., :A_KV_WIDTH].reshape((Bs, Ts) + kv_shape)], axis=1)
        vb_s = jnp.concatenate([cache_swa_v[l][:, Ts:], kv_s[..., A_KV_WIDTH:].reshape((Bs, Ts) + kv_shape)], axis=1)
        cst = convst[:, CONV_ROWS - (CONV_W - 1):]
        mk = mkv[:, :M_WIDTH].reshape(Bp, N_MEM, M_HEADS, M_HEAD_DIM)
        mv = mkv[:, M_WIDTH:].reshape(Bp, N_MEM, M_HEADS, M_HEAD_DIM)
        for lst, val in zip(outs, (kb_p, vb_p, cst[:Bp], s_fin[:Bp], mk, mv, kb_s, vb_s, cst[Bp:], s_fin[Bp:])):
            lst.append(val)

    y_prompt = rmsnorm_rows(x, ln_final, row0=0, n_rows=n_p, tm=512).reshape(Bp, Sp, D)
    y_sample = rmsnorm_rows(x, ln_final, row0=n_p, n_rows=n_s, tm=512).reshape(Bs, Ts, D)
    return (y_prompt, y_sample) + tuple(jnp.stack(o) for o in outs)
```

```python
import functools

import jax
import jax.numpy as jnp
import numpy as np
from jax import lax
from jax.experimental import pallas as pl
from jax.experimental.pallas import tpu as pltpu

F32 = jnp.float32
BF16 = jnp.bfloat16

D_MODEL = 4096
DEPTH = 4
CHUNK = 64
EPS = 1e-6
A_HEADS = 32
A_KV_HEADS = 4
A_GROUP = A_HEADS // A_KV_HEADS
A_HEAD_DIM = 64
WINDOW = 128
WIN_CHUNKS = WINDOW // CHUNK
A_WIDTH = A_HEADS * A_HEAD_DIM
A_KV_WIDTH = A_KV_HEADS * A_HEAD_DIM
B_HEADS = 16
B_DK = 128
B_DV = 128
CONV_W = 4
B_QK_WIDTH = B_HEADS * B_DK
B_WIDTH = B_HEADS * B_DV
B_CONV_CH = 2 * B_QK_WIDTH + B_WIDTH
N_MEM = 256
M_HEADS = 4
M_HEAD_DIM = 128
M_WIDTH = M_HEADS * M_HEAD_DIM
P_HEADS = 8
N_KEYS = 128
N_EXPERTS = N_KEYS * N_KEYS
P_DKEY = 256
P_HALF = P_DKEY // 2
P_TOPK = 16

LANES = 128
SUBLANES = 8

Z_QKV = 0
Z_Z = Z_QKV + B_CONV_CH
Z_AQ = Z_Z + B_WIDTH
Z_AK = Z_AQ + A_WIDTH
Z_AV = Z_AK + A_KV_WIDTH
Z_AB = Z_AV + A_KV_WIDTH
Z_USED = Z_AB + 2 * B_HEADS
Z_WIDTH = 11264

V7X_VMEM_BYTES = 64 * 1024 * 1024
VMEM_LIMIT = V7X_VMEM_BYTES - 4 * 1024 * 1024
MXU_WIDTH = 256

NT_DIMS = (((1,), (1,)), ((), ()))
TN_DIMS = (((0,), (0,)), ((), ()))


def _cparams(sem):
    return pltpu.CompilerParams(dimension_semantics=sem, vmem_limit_bytes=VMEM_LIMIT)


def _rms_rows(x, g):
    ms = jnp.mean(x * x, axis=-1, keepdims=True)
    return x * lax.rsqrt(ms + EPS) * g


def _norm_matmul_kernel(x_ref, g_ref, w_ref, o_ref, *rest, emit_hn):
    hn_ref = rest[-1]
    j = pl.program_id(1)

    @pl.when(j == 0)
    def _():
        hn = _rms_rows(x_ref[...], g_ref[...]).astype(BF16)
        hn_ref[...] = hn
        if emit_hn:
            rest[0][...] = hn.T

    o_ref[...] = jnp.dot(hn_ref[...], w_ref[...], preferred_element_type=F32)


def _layer_spec(block, index, layer, **kw):
    return pl.BlockSpec((None,) + block, lambda *ids: (layer,) + index(*ids), **kw)


def norm_matmul(x, g, w, *, layer, tm, tn, emit_hn=False):
    t, d = x.shape
    n = w.shape[2]
    out_shape = [jax.ShapeDtypeStruct((t, n), F32)]
    out_specs = [pl.BlockSpec((tm, tn), lambda i, j: (i, j))]
    if emit_hn:
        out_shape.append(jax.ShapeDtypeStruct((d, t), BF16))
        out_specs.append(pl.BlockSpec((d, tm), lambda i, j: (0, i)))
    res = pl.pallas_call(
        functools.partial(_norm_matmul_kernel, emit_hn=emit_hn),
        grid=(t // tm, n // tn),
        in_specs=[pl.BlockSpec((tm, d), lambda i, j: (i, 0)),
                  pl.BlockSpec((1, d), lambda i, j: (0, 0)),
                  _layer_spec((d, tn), lambda i, j: (0, j), layer,
                              pipeline_mode=pl.Buffered(1 if tn == n else 2))],
        out_specs=out_specs,
        out_shape=out_shape,
        scratch_shapes=[pltpu.VMEM((tm, d), BF16)],
        compiler_params=_cparams(("parallel", "arbitrary")),
        name="norm_matmul",
    )(x, g.reshape(1, d), w)
    return res if emit_hn else res[0]


def _mix_out_kernel(x_ref, a_ref, b_ref, wa_ref, wb_ref, o_ref):
    acc = jnp.dot(a_ref[...], wa_ref[...], preferred_element_type=F32)
    acc = acc + jnp.dot(b_ref[...], wb_ref[...], preferred_element_type=F32)
    o_ref[...] = x_ref[...] + acc


def mix_out_residual(x, a, b, w, *, layer, tm, tn):
    t, n = x.shape
    ka, kb = a.shape[1], b.shape[1]
    assert ka == kb and w.shape[1] == ka + kb
    return pl.pallas_call(
        _mix_out_kernel,
        grid=(t // tm, n // tn),
        in_specs=[pl.BlockSpec((tm, tn), lambda i, j: (i, j)),
                  pl.BlockSpec((tm, ka), lambda i, j: (i, 0)),
                  pl.BlockSpec((tm, kb), lambda i, j: (i, 0)),
                  _layer_spec((ka, tn), lambda i, j: (0, j), layer),
                  _layer_spec((kb, tn), lambda i, j: (1, j), layer)],
        out_specs=pl.BlockSpec((tm, tn), lambda i, j: (i, j)),
        out_shape=jax.ShapeDtypeStruct((t, n), F32),
        compiler_params=_cparams(("parallel", "arbitrary")),
        name="mix_out_residual",
    )(x, a, b, w, w)


def _rmsnorm_kernel(x_ref, yt_ref, g_ref, o_ref):
    o_ref[...] = _rms_rows(x_ref[...] + yt_ref[...].T, g_ref[...])


def rmsnorm_rows(x, yT, g, *, row0, n_rows, tm):
    d = x.shape[1]
    r0 = row0 // tm
    return pl.pallas_call(
        _rmsnorm_kernel,
        grid=(n_rows // tm,),
        in_specs=[pl.BlockSpec((tm, d), lambda i: (r0 + i, 0)),
                  pl.BlockSpec((d, tm), lambda i: (0, r0 + i)),
                  pl.BlockSpec((1, d), lambda i: (0, 0))],
        out_specs=pl.BlockSpec((tm, d), lambda i: (i, 0)),
        out_shape=jax.ShapeDtypeStruct((n_rows, d), F32),
        compiler_params=_cparams(("parallel",)),
        name="final_rmsnorm",
    )(x, yT, g.reshape(1, d))


ALIBI_SLOPES = [float(2.0 ** (-8.0 * (n + 1) / A_HEADS)) for n in range(A_HEADS)]
BAND = (WIN_CHUNKS + 1) * CHUNK


def _swa_kernel(sink_ref, q_ref, k0_ref, k1_ref, k2_ref, v0_ref, v1_ref, v2_ref, o_ref, *,
                chunks_per_seq):
    qi = lax.broadcasted_iota(jnp.int32, (CHUNK, BAND), 0)
    kj = lax.broadcasted_iota(jnp.int32, (CHUNK, BAND), 1)
    dist = jnp.abs(qi + WINDOW - kj).astype(F32)
    if chunks_per_seq is None:
        valid = None
    else:
        c = pl.program_id(0) % chunks_per_seq
        valid = kj >= (WIN_CHUNKS - jnp.minimum(c, WIN_CHUNKS)) * CHUNK
    scale = A_HEAD_DIM ** -0.5
    scores, vals = [], []
    for h in range(A_KV_HEADS):
        hs = slice(h * A_HEAD_DIM, (h + 1) * A_HEAD_DIM)
        qh = jnp.concatenate(
            [q_ref[:, (h * A_GROUP + g) * A_HEAD_DIM:(h * A_GROUP + g + 1) * A_HEAD_DIM]
             for g in range(A_GROUP)], axis=0).astype(BF16)
        kh = jnp.concatenate([k0_ref[:, hs], k1_ref[:, hs], k2_ref[:, hs]], axis=0).astype(BF16)
        vals.append(jnp.concatenate([v0_ref[:, hs], v1_ref[:, hs], v2_ref[:, hs]], axis=0).astype(BF16))
        scores.append(lax.dot_general(qh, kh, NT_DIMS, preferred_element_type=F32) * scale)
    probs = []
    for h in range(A_KV_HEADS):
        ps = []
        for g in range(A_GROUP):
            n = h * A_GROUP + g
            sg = scores[h][g * CHUNK:(g + 1) * CHUNK] - ALIBI_SLOPES[n] * dist
            if valid is not None:
                sg = jnp.where(valid, sg, -jnp.inf)
            sk = sink_ref[n]
            m = jnp.maximum(jnp.max(sg, axis=-1, keepdims=True), sk)
            p = jnp.exp(sg - m)
            p = p / (jnp.sum(p, axis=-1, keepdims=True) + jnp.exp(sk - m))
            ps.append(p.astype(BF16))
        probs.append(jnp.concatenate(ps, axis=0))
    for h in range(A_KV_HEADS):
        oh = jnp.dot(probs[h], vals[h], preferred_element_type=F32)
        for g in range(A_GROUP):
            n = h * A_GROUP + g
            o_ref[:, n * A_HEAD_DIM:(n + 1) * A_HEAD_DIM] = oh[g * CHUNK:(g + 1) * CHUNK].astype(BF16)


def swa_prompt(zz, sinks, n_seq, chunks_per_seq, total_rows):
    qb, kb, vb = Z_AQ // A_WIDTH, Z_AK // A_KV_WIDTH, Z_AV // A_KV_WIDTH

    def hist(j, col):
        def index(i):
            c = i % chunks_per_seq
            return (i - jnp.minimum(c, WIN_CHUNKS - j), col)
        return pl.BlockSpec((CHUNK, A_KV_WIDTH), index)

    return pl.pallas_call(
        functools.partial(_swa_kernel, chunks_per_seq=chunks_per_seq),
        grid=(n_seq * chunks_per_seq,),
        in_specs=[pl.BlockSpec(memory_space=pltpu.SMEM),
                  pl.BlockSpec((CHUNK, A_WIDTH), lambda i: (i, qb)),
                  hist(0, kb), hist(1, kb), hist(2, kb),
                  hist(0, vb), hist(1, vb), hist(2, vb)],
        out_specs=pl.BlockSpec((CHUNK, A_WIDTH), lambda i: (i, 0)),
        out_shape=jax.ShapeDtypeStruct((total_rows, A_WIDTH), BF16),
        compiler_params=_cparams(("parallel",)),
        name="swa_prompt",
    )(sinks, zz, zz, zz, zz, zz, zz, zz)


def swa_sample(a_out, zz, sinks, ck, cv, row_chunk0):
    n_seq = ck.shape[0] // WINDOW
    qb, kb, vb = Z_AQ // A_WIDTH, Z_AK // A_KV_WIDTH, Z_AV // A_KV_WIDTH
    cache0 = pl.BlockSpec((CHUNK, A_KV_WIDTH), lambda i: (WIN_CHUNKS * i, 0))
    cache1 = pl.BlockSpec((CHUNK, A_KV_WIDTH), lambda i: (WIN_CHUNKS * i + 1, 0))

    def body(sink_ref, q_ref, k0, k1, k2, v0, v1, v2, prev_ref, o_ref):
        del prev_ref
        _swa_kernel(sink_ref, q_ref, k0, k1, k2, v0, v1, v2, o_ref, chunks_per_seq=None)

    return pl.pallas_call(
        body,
        grid=(n_seq,),
        in_specs=[pl.BlockSpec(memory_space=pltpu.SMEM),
                  pl.BlockSpec((CHUNK, A_WIDTH), lambda i: (row_chunk0 + i, qb)),
                  cache0, cache1, pl.BlockSpec((CHUNK, A_KV_WIDTH), lambda i: (row_chunk0 + i, kb)),
                  cache0, cache1, pl.BlockSpec((CHUNK, A_KV_WIDTH), lambda i: (row_chunk0 + i, vb)),
                  pl.BlockSpec(memory_space=pl.ANY)],
        out_specs=pl.BlockSpec((CHUNK, A_WIDTH), lambda i: (row_chunk0 + i, 0)),
        out_shape=jax.ShapeDtypeStruct(a_out.shape, BF16),
        input_output_aliases={8: 0},
        compiler_params=_cparams(("parallel",)),
        name="swa_sample",
    )(sinks, zz, ck, ck, zz, cv, cv, zz, a_out)


CONV_ROWS = SUBLANES


def _silu(x):
    return x * jax.nn.sigmoid(x)


def _gdn_kernel(qkv_ref, z_ref, ab_ref, conv0_ref, s0_ref, cw_ref, alog_ref, dtb_ref, ng_ref,
                o_ref, convst_ref, s_ref, xext_ref, *, n_multi, chunks_per_seq):
    i = pl.program_id(0)
    in_multi = i < n_multi
    c = i % chunks_per_seq
    first = jnp.logical_or(jnp.logical_not(in_multi), c == 0)
    last = jnp.logical_or(jnp.logical_not(in_multi), c == chunks_per_seq - 1)

    @pl.when(first)
    def _():
        xext_ref[0:CONV_ROWS, :] = conv0_ref[0]
        s_ref[...] = s0_ref[...]

    xext_ref[CONV_ROWS:CONV_ROWS + CHUNK, :] = qkv_ref[...]

    ri = lax.broadcasted_iota(jnp.int32, (CHUNK, CHUNK), 0)
    ci = lax.broadcasted_iota(jnp.int32, (CHUNK, CHUNK), 1)
    tri_incl = ri >= ci
    tri_strict = ri > ci

    ab = ab_ref[...]
    x = ab + dtb_ref[...]
    softplus = jnp.maximum(x, 0.0) + jnp.log1p(jnp.exp(-jnp.abs(x)))
    gate = -jnp.exp(alog_ref[...]) * softplus
    beta_all = jax.nn.sigmoid(ab)
    gcum = jnp.dot(tri_incl.astype(F32), gate, preferred_element_type=F32,
                   precision=lax.Precision.HIGHEST)
    gcum_t = gcum.T
    egc = jnp.exp(gcum)
    glast = gcum[CHUNK - 1:CHUNK, :]
    etail = jnp.exp(glast - gcum)
    eglast = jnp.exp(glast)

    def conv(col0):
        cs = slice(col0, col0 + LANES)
        y = xext_ref[CONV_ROWS:CONV_ROWS + CHUNK, cs] * cw_ref[CONV_W - 1:CONV_W, cs]
        for j in range(1, CONV_W):
            y = y + xext_ref[CONV_ROWS - j:CONV_ROWS - j + CHUNK, cs] * cw_ref[CONV_W - 1 - j:CONV_W - j, cs]
        return _silu(y)

    def l2n(a):
        return a * lax.rsqrt(jnp.sum(a * a, axis=-1, keepdims=True) + EPS)

    heads = range(B_HEADS)
    qs, ks, attns, xss, ms = [], [], [], [], []
    for h in heads:
        q = l2n(conv(h * B_DK)) * (B_DK ** -0.5)
        k = l2n(conv(B_QK_WIDTH + h * B_DK))
        v = conv(2 * B_QK_WIDTH + h * B_DV)
        beta = beta_all[:, B_HEADS + h:B_HEADS + h + 1]
        gc_col = gcum[:, h:h + 1]
        gc_row = gcum_t[h:h + 1, :]
        decay = jnp.exp(jnp.where(tri_incl, gc_col - gc_row, -jnp.inf))
        kbeta = k * beta
        kk_qk = lax.dot_general(jnp.concatenate([kbeta, q], axis=0).astype(BF16), k.astype(BF16),
                                NT_DIMS, preferred_element_type=F32)
        ms.append(-jnp.where(tri_strict, kk_qk[:CHUNK] * decay, 0.0))
        attns.append((kk_qk[CHUNK:] * decay).astype(BF16))
        xss.append(jnp.concatenate([v * beta, kbeta * egc[:, h:h + 1]], axis=1))
        qs.append((q * egc[:, h:h + 1]).astype(BF16))
        ks.append((k * etail[:, h:h + 1]).astype(BF16))
    for _ in range(6):
        for h in heads:
            mb = ms[h].astype(BF16)
            y = jnp.dot(mb, jnp.concatenate([xss[h].astype(BF16), mb], axis=1), preferred_element_type=F32)
            xss[h] = xss[h] + y[:, :2 * B_DV]
            ms[h] = y[:, 2 * B_DV:]
    v_news, os_ = [], []
    for h in heads:
        u, w = xss[h][:, :B_DV], xss[h][:, B_DV:]
        ws_qs = jnp.dot(jnp.concatenate([w.astype(BF16), qs[h]], axis=0),
                        s_ref[0, h].astype(BF16), preferred_element_type=F32)
        v_news.append((u - ws_qs[:CHUNK]).astype(BF16))
        os_.append(ws_qs[CHUNK:])
    for h in heads:
        o = os_[h] + jnp.dot(attns[h], v_news[h], preferred_element_type=F32)
        s_ref[0, h] = s_ref[0, h] * eglast[:, h:h + 1] + lax.dot_general(
            ks[h], v_news[h], TN_DIMS, preferred_element_type=F32)
        zs = slice(h * B_DV, (h + 1) * B_DV)
        o_ref[:, zs] = (_rms_rows(o, ng_ref[...]) * _silu(z_ref[:, zs])).astype(BF16)

    xext_ref[0:CONV_ROWS, :] = xext_ref[CHUNK:CHUNK + CONV_ROWS, :]

    @pl.when(last)
    def _():
        convst_ref[0] = xext_ref[0:CONV_ROWS, :]


def gdn_mixer(zz, conv0, s0, conv_w, a_log, dt_bias, norm_g, *, n_multi_seq, chunks_per_seq):
    t = zz.shape[0]
    n_seq = s0.shape[0]
    n_multi = n_multi_seq * chunks_per_seq
    n_steps = t // CHUNK
    assert n_steps == n_multi + (n_seq - n_multi_seq)

    def seq(i):
        return jnp.where(i < n_multi, i // chunks_per_seq, i - n_multi + n_multi_seq)

    pad = lambda a: jnp.pad(a.astype(F32), (0, LANES - a.shape[0])).reshape(1, LANES)
    kern = functools.partial(_gdn_kernel, n_multi=n_multi, chunks_per_seq=chunks_per_seq)
    return pl.pallas_call(
        kern,
        grid=(n_steps,),
        in_specs=[pl.BlockSpec((CHUNK, B_CONV_CH), lambda i: (i, Z_QKV // B_CONV_CH)),
                  pl.BlockSpec((CHUNK, B_WIDTH), lambda i: (i, Z_Z // B_WIDTH)),
                  pl.BlockSpec((CHUNK, LANES), lambda i: (i, Z_AB // LANES)),
                  pl.BlockSpec((1, CONV_ROWS, B_CONV_CH), lambda i: (seq(i), 0, 0)),
                  pl.BlockSpec((1, B_HEADS, B_DK, B_DV), lambda i: (seq(i), 0, 0, 0)),
                  pl.BlockSpec((CONV_W, B_CONV_CH), lambda i: (0, 0)),
                  pl.BlockSpec((1, LANES), lambda i: (0, 0)),
                  pl.BlockSpec((1, LANES), lambda i: (0, 0)),
                  pl.BlockSpec((1, B_DV), lambda i: (0, 0))],
        out_specs=[pl.BlockSpec((CHUNK, B_WIDTH), lambda i: (i, 0)),
                   pl.BlockSpec((1, CONV_ROWS, B_CONV_CH), lambda i: (seq(i), 0, 0)),
                   pl.BlockSpec((1, B_HEADS, B_DK, B_DV), lambda i: (seq(i), 0, 0, 0))],
        out_shape=[jax.ShapeDtypeStruct((t, B_WIDTH), BF16),
                   jax.ShapeDtypeStruct((n_seq, CONV_ROWS, B_CONV_CH), F32),
                   jax.ShapeDtypeStruct((n_seq, B_HEADS, B_DK, B_DV), F32)],
        scratch_shapes=[pltpu.VMEM((CONV_ROWS + CHUNK, B_CONV_CH), F32)],
        compiler_params=_cparams(("arbitrary",)),
        name="gdn_mixer",
    )(zz, zz, zz, conv0, s0, conv_w, pad(a_log), pad(dt_bias), norm_g.reshape(1, B_DV))


def _cross_kernel(x_ref, g_ref, wq_ref, mk_ref, mv_ref, wo_ref, o_ref, *, n_batch):
    x = x_ref[...]
    tm = x.shape[0]
    rows = tm // n_batch
    hn = _rms_rows(x, g_ref[...]).astype(BF16)
    q = jnp.dot(hn, wq_ref[...], preferred_element_type=F32)
    scale = M_HEAD_DIM ** -0.5
    pairs = [(b, h) for b in range(n_batch) for h in range(M_HEADS)]
    scores, vals = {}, {}
    for b, h in pairs:
        hs = slice(h * M_HEAD_DIM, (h + 1) * M_HEAD_DIM)
        qh = q[b * rows:(b + 1) * rows, hs].astype(BF16)
        kh = mk_ref[b * N_MEM:(b + 1) * N_MEM, hs].astype(BF16)
        vals[b, h] = mv_ref[b * N_MEM:(b + 1) * N_MEM, hs].astype(BF16)
        scores[b, h] = lax.dot_general(qh, kh, NT_DIMS, preferred_element_type=F32) * scale
    probs = {}
    for b, h in pairs:
        s = scores[b, h]
        p = jnp.exp(s - jnp.max(s, axis=-1, keepdims=True))
        probs[b, h] = (p / jnp.sum(p, axis=-1, keepdims=True)).astype(BF16)
    outs = []
    for b in range(n_batch):
        outs.append(jnp.concatenate(
            [jnp.dot(probs[b, h], vals[b, h], preferred_element_type=F32) for h in range(M_HEADS)], axis=1))
    o = jnp.concatenate(outs, axis=0) if n_batch > 1 else outs[0]
    o_ref[...] = x + jnp.dot(o.astype(BF16), wo_ref[...], preferred_element_type=F32)


def cross_block(x, g, wq, mk, mv, mk_col, mv_col, wo, *, layer, row0, n_rows, rows_per_batch, tm):
    t, d = x.shape
    if rows_per_batch >= tm:
        n_batch = 1
        tiles_per_batch = rows_per_batch // tm
        mem_index = lambda col: (lambda i: (i // tiles_per_batch, col))
    else:
        n_batch = tm // rows_per_batch
        mem_index = lambda col: (lambda i: (i, col))
    r0 = row0 // tm
    return pl.pallas_call(
        functools.partial(_cross_kernel, n_batch=n_batch),
        grid=(n_rows // tm,),
        in_specs=[pl.BlockSpec((tm, d), lambda i: (r0 + i, 0)),
                  pl.BlockSpec((1, d), lambda i: (0, 0)),
                  _layer_spec((d, M_WIDTH), lambda i: (0, 0), layer),
                  pl.BlockSpec((n_batch * N_MEM, M_WIDTH), mem_index(mk_col)),
                  pl.BlockSpec((n_batch * N_MEM, M_WIDTH), mem_index(mv_col)),
                  _layer_spec((M_WIDTH, d), lambda i: (0, 0), layer)],
        out_specs=pl.BlockSpec((tm, d), lambda i: (r0 + i, 0)),
        out_shape=jax.ShapeDtypeStruct((t, d), F32),
        input_output_aliases={0: 0},
        compiler_params=_cparams(("parallel",)),
        name="cross_block",
    )(x, g.reshape(1, d), wq, mk, mv, wo)


def _batcher_network(n):
    pairs = []
    t = n.bit_length() - 1
    p = 1 << (t - 1)
    while p > 0:
        q, r, d = 1 << (t - 1), 0, p
        while d > 0:
            for i in range(n - d):
                if i & p == r:
                    pairs.append((i, i + d))
            d, q, r = q - p, q >> 1, p
        p >>= 1
    return pairs


SORT16 = _batcher_network(P_TOPK)


def _cmpx(vals, i, j):
    vals[i], vals[j] = jnp.maximum(vals[i], vals[j]), jnp.minimum(vals[i], vals[j])


def _sort_bitonic(vals):
    stride = P_TOPK // 2
    while stride:
        for i in range(P_TOPK):
            if (i // stride) % 2 == 0:
                _cmpx(vals, i, i + stride)
        stride //= 2
    return vals


def _merge_sublanes(vals):
    for shift in (4, 2, 1):
        other = [pltpu.roll(v, shift, axis=0) for v in vals]
        vals = _sort_bitonic([jnp.maximum(vals[i], other[P_TOPK - 1 - i]) for i in range(P_TOPK)])
    return vals


def _peer_select_kernel(q_ref, sk1_ref, sk2_ref, s1_ref, s2_ref, tau_ref, m_ref, iz_ref):
    tm = q_ref.shape[0]
    sub = lax.broadcasted_iota(jnp.int32, (SUBLANES, tm), 0)
    neg = jnp.full((SUBLANES, tm), -jnp.inf, F32)

    def top16(s):
        vals = [s[SUBLANES * j:SUBLANES * (j + 1)] for j in range(N_KEYS // SUBLANES)]
        for i, j in SORT16:
            _cmpx(vals, i, j)
        return _merge_sublanes(vals)

    def spread(vals):
        out = vals[0]
        for b in range(1, SUBLANES):
            out = jnp.where(sub == b, vals[b], out)
        return out

    for h in range(P_HEADS):
        q1 = q_ref[:, h * P_DKEY:h * P_DKEY + P_HALF].astype(BF16)
        q2 = q_ref[:, h * P_DKEY + P_HALF:(h + 1) * P_DKEY].astype(BF16)
        s1 = lax.dot_general(sk1_ref[h], q1, NT_DIMS, preferred_element_type=F32)
        s2 = lax.dot_general(sk2_ref[h], q2, NT_DIMS, preferred_element_type=F32)
        s1_ref[h] = s1
        s2_ref[h] = s2
        t1 = top16(s1)
        t2 = top16(s2)
        t2_lo, t2_hi = spread(t2[:SUBLANES]), spread(t2[SUBLANES:])
        cand = [jnp.where((a + 1) * (sub + 1) <= P_TOPK, t1[a] + t2_lo, neg) for a in range(P_TOPK)]
        cand[P_TOPK - 1] = jnp.maximum(cand[P_TOPK - 1], t1[0] + t2_hi)
        best = _merge_sublanes(_sort_bitonic(cand))
        m = best[0][0:1, :]
        z = jnp.zeros((1, tm), F32)
        for k in range(P_TOPK):
            z = z + jnp.exp(best[k][0:1, :] - m)
        tau_ref[h:h + 1, :] = best[P_TOPK - 1][0:1, :]
        m_ref[h:h + 1, :] = m
        iz_ref[h:h + 1, :] = 1.0 / z


def peer_select(q, sk1, sk2, *, tm):
    t = q.shape[0]
    big = jax.ShapeDtypeStruct((P_HEADS, N_KEYS, t), F32)
    small = jax.ShapeDtypeStruct((P_HEADS, t), F32)
    big_spec = pl.BlockSpec((P_HEADS, N_KEYS, tm), lambda i: (0, 0, i))
    small_spec = pl.BlockSpec((P_HEADS, tm), lambda i: (0, i))
    key_spec = pl.BlockSpec((P_HEADS, N_KEYS, P_HALF), lambda i: (0, 0, 0))
    return pl.pallas_call(
        _peer_select_kernel,
        grid=(t // tm,),
        in_specs=[pl.BlockSpec((tm, P_HEADS * P_DKEY), lambda i: (i, 0)), key_spec, key_spec],
        out_specs=[big_spec, big_spec, small_spec, small_spec, small_spec],
        out_shape=[big, big, small, small, small],
        compiler_params=_cparams(("parallel",)),
        name="peer_select",
    )(q, sk1, sk2)


SQRT_HALF = float(np.sqrt(0.5))


def _peer_main_kernel(hnT_ref, u_ref, vt_ref, s1_ref, s2_ref, tau_ref, m_ref, iz_ref, o_ref,
                      a0_ref, a1_ref, p_ref):
    e = pl.program_id(1)
    te, tm = a0_ref.shape
    rows = te // N_KEYS
    pieces = [slice(t0, t0 + MXU_WIDTH) for t0 in range(0, tm, MXU_WIDTH)]
    prev_tile = jnp.maximum(e - 1, 0)

    @pl.when(e == 0)
    def _():
        o_ref[...] = jnp.zeros_like(o_ref)
        a1_ref[...] = jnp.zeros_like(a1_ref)

    def step(a_cur, a_prev):
        for hs in pieces:
            a_cur[:, hs] = jnp.dot(u_ref[...], hnT_ref[:, hs], preferred_element_type=F32)
            for r in range(rows):
                i1 = prev_tile * rows + r
                rs = slice(r * N_KEYS, (r + 1) * N_KEYS)
                for c in range(hs.start // LANES, hs.stop // LANES):
                    cs = slice(c * LANES, (c + 1) * LANES)
                    g = jnp.zeros((N_KEYS, LANES), F32)
                    for h in range(P_HEADS):
                        s = s1_ref[i1, h:h + 1, cs] + s2_ref[h, :, cs]
                        w = jnp.exp(s - m_ref[h:h + 1, cs]) * iz_ref[h:h + 1, cs]
                        g = g + jnp.where(s >= tau_ref[h:h + 1, cs], w, 0.0)
                    a = a_prev[rs, cs]
                    act = 0.5 * a * (1.0 + lax.erf(a * SQRT_HALF))
                    p_ref[rs, cs] = (act * g).astype(BF16)
            o_ref[:, hs] += jnp.dot(vt_ref[...], p_ref[:, hs], preferred_element_type=F32)

    @pl.when(e % 2 == 0)
    def _():
        step(a0_ref, a1_ref)

    @pl.when(e % 2 == 1)
    def _():
        step(a1_ref, a0_ref)


def peer_main(hnT, u, vt, s1, s2, tau, m, iz, *, layer, tm, te):
    d, t = hnT.shape
    n_e = N_EXPERTS // te
    once = pl.Buffered(1)
    small_spec = pl.BlockSpec((P_HEADS, tm), lambda i, e: (0, i))
    big_spec = pl.BlockSpec((P_HEADS, N_KEYS, tm), lambda i, e: (0, 0, i), pipeline_mode=once)
    s1_spec = pl.BlockSpec((N_KEYS, P_HEADS, tm), lambda i, e: (0, 0, i), pipeline_mode=once)
    return pl.pallas_call(
        _peer_main_kernel,
        grid=(t // tm, n_e + 1),
        in_specs=[pl.BlockSpec((d, tm), lambda i, e: (0, i), pipeline_mode=once),
                  _layer_spec((te, d), lambda i, e: (jnp.minimum(e, n_e - 1), 0), layer),
                  _layer_spec((d, te), lambda i, e: (0, jnp.maximum(e - 1, 0)), layer),
                  s1_spec, big_spec, small_spec, small_spec, small_spec],
        out_specs=pl.BlockSpec((d, tm), lambda i, e: (0, i)),
        out_shape=jax.ShapeDtypeStruct((d, t), F32),
        scratch_shapes=[pltpu.VMEM((te, tm), F32), pltpu.VMEM((te, tm), F32), pltpu.VMEM((te, tm), BF16)],
        compiler_params=_cparams(("parallel", "arbitrary")),
        name="peer_main",
    )(hnT, u, vt, jnp.transpose(s1, (1, 0, 2)), s2, tau, m, iz)


def _residual_t_kernel(x_ref, yt_ref, o_ref):
    o_ref[...] = x_ref[...] + yt_ref[...].T


def residual_add_transposed(x, yT, *, tm):
    t, d = x.shape
    return pl.pallas_call(
        _residual_t_kernel,
        grid=(t // tm,),
        in_specs=[pl.BlockSpec((tm, d), lambda i: (i, 0)),
                  pl.BlockSpec((d, tm), lambda i: (0, i))],
        out_specs=pl.BlockSpec((tm, d), lambda i: (i, 0)),
        out_shape=jax.ShapeDtypeStruct((t, d), F32),
        input_output_aliases={0: 0},
        compiler_params=_cparams(("parallel",)),
        name="residual_add_transposed",
    )(x, yT)


def kernel(x_prompt, x_sample, mem_prompt, cache_swa_k, cache_swa_v, state_conv, state_gdn, cache_mem_k, cache_mem_v, ln_mix, w_in, conv_w, a_log, dt_bias, gdn_norm, sinks, w_out, ln_cross, ln_mem, w_mq, w_mk, w_mv, w_mo, ln_ffn, w_pq, sub_keys1, sub_keys2, expert_u, expert_v, ln_final):
    Bp, Sp, D = x_prompt.shape
    Bs, Ts, _ = x_sample.shape
    assert Ts == CHUNK and Sp % CHUNK == 0 and cache_swa_k.shape[2] == WINDOW
    n_p = Bp * Sp
    n_s = Bs * Ts
    n_tok = n_p + n_s
    cps = Sp // CHUNK
    x = jnp.concatenate([x_prompt.reshape(n_p, D), x_sample.reshape(n_s, D)], axis=0)
    mem = mem_prompt.reshape(Bp * N_MEM, D)
    conv_pad = ((0, 0), (CONV_ROWS - (CONV_W - 1), 0), (0, 0))

    o_qkv = A_WIDTH + 2 * A_KV_WIDTH
    o_ab = o_qkv + B_CONV_CH
    o_z = o_ab + 2 * B_HEADS
    w_in_all = jnp.concatenate(
        [w_in[:, :, o_qkv:o_ab].astype(BF16), w_in[:, :, o_z:].astype(BF16), w_in[:, :, :o_qkv].astype(BF16),
         w_in[:, :, o_ab:o_z].astype(BF16), jnp.zeros((DEPTH, D, Z_WIDTH - Z_USED), BF16)], axis=2)
    w_out_all = w_out.astype(BF16)
    w_mq_all = w_mq.astype(BF16)
    w_mkv_all = jnp.concatenate([w_mk, w_mv], axis=2).astype(BF16)
    w_mo_all = w_mo.astype(BF16)
    w_pq_all = w_pq.astype(BF16)
    u_all = expert_u.astype(BF16)
    vt_all = jnp.swapaxes(expert_v.astype(BF16), 1, 2)

    outs = [[] for _ in range(10)]
    for l in range(DEPTH):
        sk1_l = sub_keys1[l].astype(BF16)
        sk2_l = sub_keys2[l].astype(BF16)

        zz = norm_matmul(x, ln_mix[l], w_in_all, layer=l, tm=512, tn=1024)
        ck = cache_swa_k[l].reshape(Bs * WINDOW, A_KV_WIDTH)
        cv = cache_swa_v[l].reshape(Bs * WINDOW, A_KV_WIDTH)
        a_out = swa_prompt(zz, sinks[l], Bp, cps, n_tok)
        a_out = swa_sample(a_out, zz, sinks[l], ck, cv, n_p // CHUNK)
        conv0 = jnp.concatenate([jnp.zeros((Bp, CONV_ROWS, B_CONV_CH), F32),
                                 jnp.pad(state_conv[l], conv_pad)], axis=0)
        s0 = jnp.concatenate([jnp.zeros((Bp, B_HEADS, B_DK, B_DV), F32), state_gdn[l]], axis=0)
        b_out, convst, s_fin = gdn_mixer(zz, conv0, s0, conv_w[l], a_log[l], dt_bias[l], gdn_norm[l],
                                         n_multi_seq=Bp, chunks_per_seq=cps)
        x = mix_out_residual(x, a_out, b_out, w_out_all, layer=l, tm=512, tn=1024)

        mkv = norm_matmul(mem, ln_mem[l], w_mkv_all, layer=l, tm=512, tn=2 * M_WIDTH)
        x = cross_block(x, ln_cross[l], w_mq_all, mkv, mkv, 0, 1, w_mo_all,
                        layer=l, row0=0, n_rows=n_p, rows_per_batch=Sp, tm=256)
        cmk = cache_mem_k[l].reshape(Bs * N_MEM, M_WIDTH)
        cmv = cache_mem_v[l].reshape(Bs * N_MEM, M_WIDTH)
        x = cross_block(x, ln_cross[l], w_mq_all, cmk, cmv, 0, 0, w_mo_all,
                        layer=l, row0=n_p, n_rows=n_s, rows_per_batch=Ts, tm=256)

        qp, hnT = norm_matmul(x, ln_ffn[l], w_pq_all, layer=l, tm=512, tn=P_HEADS * P_DKEY, emit_hn=True)
        s1, s2, tau, m, iz = peer_select(qp, sk1_l, sk2_l, tm=256)
        yT = peer_main(hnT, u_all, vt_all, s1, s2, tau, m, iz, layer=l, tm=768, te=512)
        if l + 1 < DEPTH:
            x = residual_add_transposed(x, yT, tm=256)

        kv_shape = (A_KV_HEADS, A_HEAD_DIM)
        kv_p = lax.slice(zz, (0, Z_AK), (n_p, Z_AB)).reshape(Bp, Sp, 2 * A_KV_WIDTH)[:, Sp - WINDOW:]
        kv_s = lax.slice(zz, (n_p, Z_AK), (n_tok, Z_AB)).reshape(Bs, Ts, 2 * A_KV_WIDTH)
        kb_p = kv_p[..., :A_KV_WIDTH].reshape((Bp, WINDOW) + kv_shape)
        vb_p = kv_p[..., A_KV_WIDTH:].reshape((Bp, WINDOW) + kv_shape)
        kb_s = jnp.concatenate([cache_swa_k[l][:, Ts:], kv_s[..., :A_KV_WIDTH].reshape((Bs, Ts) + kv_shape)], axis=1)
        vb_s = jnp.concatenate([cache_swa_v[l][:, Ts:], kv_s[..., A_KV_WIDTH:].reshape((Bs, Ts) + kv_shape)], axis=1)
        cst = convst[:, CONV_ROWS - (CONV_W - 1):]
        mk = mkv[:, :M_WIDTH].reshape(Bp, N_MEM, M_HEADS, M_HEAD_DIM)
        mv = mkv[:, M_WIDTH:].reshape(Bp, N_MEM, M_HEADS, M_HEAD_DIM)
        for lst, val in zip(outs, (kb_p, vb_p, cst[:Bp], s_fin[:Bp], mk, mv, kb_s, vb_s, cst[Bp:], s_fin[Bp:])):
            lst.append(val)

    y_prompt = rmsnorm_rows(x, yT, ln_final, row0=0, n_rows=n_p, tm=256).reshape(Bp, Sp, D)
    y_sample = rmsnorm_rows(x, yT, ln_final, row0=n_p, n_rows=n_s, tm=256).reshape(Bs, Ts, D)
    return (y_prompt, y_sample) + tuple(jnp.stack(o) for o in outs)
```
